```python
import math
import jax, jax.numpy as jnp
from jax import lax
import numpy as np


D_MODEL = 4096
BATCH = 2
SEQ = 8192
DEPTH = 1
DEC_BATCH = 2
DEC_SEQ = 4096
PAST_LEN = 128

N_META = 16
GRID_W = 64
H_A = 8
HD_A = 128
W_A = H_A * 2 * HD_A
Q_BLOCK = 128
REL_BUCKETS = 32
REL_MAX_DIST = 128
H_B = 16
HD_B = 128
W_B = H_B * HD_B
NA_MAX_ROWS = 8
NA_COLS = 16
N_EXPERTS = 32
TOP_K = 4
D_FF = 4096
SWIGLU_LIMIT = 7.0
SWIGLU_ALPHA = 1.702
MOE_BLOCK = 256
EPS = 1e-6
D_IN = 3 * W_A + 3 * W_B + 2 * D_MODEL

kernel_name = 'hybrid_diffattn_natten_moe_encoder'


def _rmsnorm(x, g):
    xf = x.astype(jnp.float32)
    y = xf * lax.rsqrt(jnp.mean(xf * xf, axis=-1, keepdims=True) + EPS)
    return (y * g.astype(jnp.float32)).astype(x.dtype)


def _t5_bucket(rel):
    half = REL_BUCKETS // 2
    max_exact = half // 2
    sign = (rel > 0).astype(jnp.int32) * half
    n = jnp.abs(rel)
    nf = jnp.maximum(n, 1).astype(jnp.float32)
    large = max_exact + (jnp.log(nf / max_exact) / math.log(REL_MAX_DIST / max_exact) * (half - max_exact)).astype(jnp.int32)
    large = jnp.minimum(large, half - 1)
    return sign + jnp.where(n < max_exact, n, large)


def _diff_attention(q, k, v, lam, rel_bias):
    b, h, L = q.shape[:3]
    s_real = L - N_META
    scale = HD_A ** -0.5
    q1, q2 = q[..., 0, :], q[..., 1, :]
    k1, k2 = k[..., 0, :], k[..., 1, :]
    k_pos = jnp.arange(L, dtype=jnp.int32)

    def attend(args):
        q1b, q2b, q_pos = args
        bias = rel_bias[_t5_bucket(k_pos[None, :] - q_pos[:, None])]
        bias = jnp.transpose(bias, (2, 0, 1)).astype(jnp.float32)[None]
        s1 = jnp.einsum('bhqd,bhkd->bhqk', q1b, k1).astype(jnp.float32) * scale + bias
        s2 = jnp.einsum('bhqd,bhkd->bhqk', q2b, k2).astype(jnp.float32) * scale + bias
        p = jax.nn.softmax(s1, axis=-1) - lam * jax.nn.softmax(s2, axis=-1)
        return jnp.einsum('bhqk,bhkd->bhqd', p.astype(v.dtype), v)

    out_meta = attend((q1[:, :, :N_META], q2[:, :, :N_META], jnp.arange(N_META, dtype=jnp.int32)))
    nqb = s_real // Q_BLOCK

    def blocks(t):
        return jnp.moveaxis(t[:, :, N_META:].reshape(b, h, nqb, Q_BLOCK, HD_A), 2, 0)

    pos = (N_META + jnp.arange(s_real, dtype=jnp.int32)).reshape(nqb, Q_BLOCK)
    out_real = lax.map(attend, (blocks(q1), blocks(q2), pos))
    out_real = jnp.moveaxis(out_real, 0, 2).reshape(b, h, s_real, 2 * HD_A)
    return jnp.concatenate([out_meta, out_real], axis=2)


def _neighborhood_attention(q, k, v, rpb):
    b, h, L, d = q.shape
    s_real = L - N_META
    rows = s_real // GRID_W
    kh = min(NA_MAX_ROWS, rows)
    kw = NA_COLS
    scale = HD_B ** -0.5
    qm, km, vm = q[:, :, :N_META], k[:, :, :N_META], v[:, :, :N_META]

    def grid(t):
        return t[:, :, N_META:].reshape(b, h, rows, GRID_W, d)

    qg, kg, vg = grid(q), grid(k), grid(v)
    pm = jax.nn.softmax(jnp.einsum('bhqd,bhkd->bhqk', qm, km).astype(jnp.float32) * scale, axis=-1)
    out_meta = jnp.einsum('bhqk,bhkd->bhqd', pm.astype(v.dtype), vm)

    cols = jnp.arange(GRID_W, dtype=jnp.int32)
    col_start = jnp.clip(cols - kw // 2, 0, GRID_W - kw)
    col_idx = col_start[:, None] + jnp.arange(kw, dtype=jnp.int32)[None, :]
    col_off = col_idx - cols[:, None]

    def row_step(r):
        rs = jnp.clip(r - kh // 2, 0, rows - kh)
        k_rows = lax.dynamic_slice_in_dim(kg, rs, kh, axis=2)
        v_rows = lax.dynamic_slice_in_dim(vg, rs, kh, axis=2)
        k_win = k_rows[:, :, :, col_idx]
        v_win = v_rows[:, :, :, col_idx]
        q_r = lax.dynamic_index_in_dim(qg, r, axis=2, keepdims=False)
        row_off = rs + jnp.arange(kh, dtype=jnp.int32) - r
        bias = rpb[:, row_off[None, :, None] + (NA_MAX_ROWS - 1), col_off[:, None, :] + (NA_COLS - 1)]
        s_win = jnp.einsum('bhcd,bhacjd->bhcaj', q_r, k_win).astype(jnp.float32) * scale + bias[None].astype(jnp.float32)
        s_meta = jnp.einsum('bhcd,bhmd->bhcm', q_r, km).astype(jnp.float32) * scale
        s = jnp.concatenate([s_win.reshape(b, h, GRID_W, kh * kw), s_meta], axis=-1)
        p = jax.nn.softmax(s, axis=-1).astype(v.dtype)
        p_win = p[..., :kh * kw].reshape(b, h, GRID_W, kh, kw)
        p_meta = p[..., kh * kw:]
        return jnp.einsum('bhcaj,bhacjd->bhcd', p_win, v_win) + jnp.einsum('bhcm,bhmd->bhcd', p_meta, vm)

    out_real = lax.map(row_step, jnp.arange(rows, dtype=jnp.int32))
    out_real = jnp.moveaxis(out_real, 0, 2).reshape(b, h, s_real, d)
    return jnp.concatenate([out_meta, out_real], axis=2)


def _moe(x, w_router, b_router, w_gate, b_gate, w_up, b_up, w_down, b_down):
    n, dm = x.shape
    logits = jnp.dot(x, w_router).astype(jnp.float32) + b_router.astype(jnp.float32)
    top_v, top_i = lax.top_k(logits, TOP_K)
    gate_w = jax.nn.softmax(top_v, axis=-1)
    nk = n * TOP_K
    flat_e = top_i.reshape(-1)
    flat_tok = jnp.repeat(jnp.arange(n, dtype=jnp.int32), TOP_K)
    flat_w = gate_w.reshape(-1)
    order = jnp.argsort(flat_e)
    se, stok, sw = flat_e[order], flat_tok[order], flat_w[order]
    counts = jnp.bincount(flat_e, length=N_EXPERTS)
    offsets = jnp.cumsum(counts) - counts
    nblk = (counts + MOE_BLOCK - 1) // MOE_BLOCK
    blk_end = jnp.cumsum(nblk)
    blk_start = blk_end - nblk
    dest = blk_start[se] * MOE_BLOCK + (jnp.arange(nk, dtype=jnp.int32) - offsets[se])
    nb = -(-nk // MOE_BLOCK) + N_EXPERTS
    slot_tok = jnp.full((nb * MOE_BLOCK,), n, dtype=jnp.int32).at[dest].set(stok)
    slot_w = jnp.zeros((nb * MOE_BLOCK,), x.dtype).at[dest].set(sw.astype(x.dtype))
    block_e = jnp.clip(jnp.searchsorted(blk_end, jnp.arange(nb), side='right'), 0, N_EXPERTS - 1)
    x_pad = jnp.concatenate([x, jnp.zeros((1, dm), x.dtype)], axis=0)

    def step(acc, xs):
        tok, w, e = xs
        xb = x_pad[tok]
        g = jnp.dot(xb, w_gate[e]) + b_gate[e]
        u = jnp.dot(xb, w_up[e]) + b_up[e]
        g = jnp.minimum(g, SWIGLU_LIMIT)
        u = jnp.clip(u, -SWIGLU_LIMIT, SWIGLU_LIMIT)
        y = jnp.dot((u + 1.0) * (g * jax.nn.sigmoid(SWIGLU_ALPHA * g)), w_down[e]) + b_down[e]
        return acc.at[tok].add(y * w[:, None]), None

    acc0 = jnp.zeros((n + 1, dm), x.dtype)
    acc, _ = lax.scan(step, acc0, (slot_tok.reshape(nb, MOE_BLOCK), slot_w.reshape(nb, MOE_BLOCK), block_e))
    return acc[:n]


def _encoder_layer(h, l, p):
    b, L, dm = h.shape
    hn = _rmsnorm(h, p['norm_mix'][l])
    proj = jnp.dot(hn, p['w_in'][l])
    splits = [W_A, 2 * W_A, 3 * W_A, 3 * W_A + W_B, 3 * W_A + 2 * W_B, 3 * W_A + 3 * W_B, 3 * W_A + 3 * W_B + D_MODEL]
    qa, ka, va, qb, kb, vb, ga, gb = jnp.split(proj, splits, axis=-1)
    qa = jnp.transpose(_rmsnorm(qa.reshape(b, L, H_A, 2, HD_A), p['qk_norm_a_q'][l]), (0, 2, 1, 3, 4))
    ka = jnp.transpose(_rmsnorm(ka.reshape(b, L, H_A, 2, HD_A), p['qk_norm_a_k'][l]), (0, 2, 1, 3, 4))
    va = jnp.transpose(va.reshape(b, L, H_A, 2 * HD_A), (0, 2, 1, 3))
    lambda_init = 0.8 - 0.6 * math.exp(-0.3 * l)
    lam = (jnp.exp(jnp.sum(p['lambda_q1'][l].astype(jnp.float32) * p['lambda_k1'][l].astype(jnp.float32)))
           - jnp.exp(jnp.sum(p['lambda_q2'][l].astype(jnp.float32) * p['lambda_k2'][l].astype(jnp.float32)))
           + lambda_init)
    oa = _diff_attention(qa, ka, va, lam, p['rel_bias'])
    oa = _rmsnorm(oa, p['subln_a'][l]) * (1.0 - lambda_init)
    oa = jnp.transpose(oa, (0, 2, 1, 3)).reshape(b, L, W_A)
    qb = jnp.transpose(_rmsnorm(qb.reshape(b, L, H_B, HD_B), p['qk_norm_b_q'][l]), (0, 2, 1, 3))
    kb = jnp.transpose(_rmsnorm(kb.reshape(b, L, H_B, HD_B), p['qk_norm_b_k'][l]), (0, 2, 1, 3))
    vb = jnp.transpose(vb.reshape(b, L, H_B, HD_B), (0, 2, 1, 3))
    ob = _neighborhood_attention(qb, kb, vb, p['na_rpb'][l])
    ob = jnp.transpose(ob, (0, 2, 1, 3)).reshape(b, L, W_B)
    merged = jax.nn.sigmoid(ga) * jnp.dot(oa, p['w_br_a'][l]) + jax.nn.sigmoid(gb) * jnp.dot(ob, p['w_br_b'][l])
    h = h + jnp.dot(merged, p['w_out'][l])
    hf = _rmsnorm(h, p['norm_ffn'][l]).reshape(b * L, dm)
    y = _moe(hf, p['w_router'][l], p['b_router'][l], p['w_gate'][l], p['b_gate'][l],
             p['w_up'][l], p['b_up'][l], p['w_down'][l], p['b_down'][l])
    return h + y.reshape(b, L, dm)


def _trunk(x, p):
    b = x.shape[0]
    meta = jnp.broadcast_to(p['meta_tokens'].astype(x.dtype)[None], (b, N_META, D_MODEL))
    h = jnp.concatenate([meta, x], axis=1)
    for l in range(DEPTH):
        h = _encoder_layer(h, l, p)
    return h[:, N_META:]


def setup_inputs(seed: int = 0) -> dict:
    key = jax.random.key(seed)
    ks = jax.random.split(key, 28)

    def nrm(k, shape, scale):
        return jax.random.normal(k, shape, jnp.float32) * scale

    def gain(k, shape):
        return 1.0 + nrm(k, shape, 0.01)

    return {
        'x_prompt': nrm(ks[0], (BATCH, SEQ, D_MODEL), 1.0),
        'x_sample': nrm(ks[1], (DEC_BATCH, DEC_SEQ, D_MODEL), 1.0),
        'meta_tokens': nrm(ks[2], (N_META, D_MODEL), 1.0),
        'norm_mix': gain(ks[3], (DEPTH, D_MODEL)),
        'w_in': nrm(ks[4], (DEPTH, D_MODEL, D_IN), D_MODEL ** -0.5),
        'qk_norm_a_q': gain(ks[5], (DEPTH, HD_A)),
        'qk_norm_a_k': gain(ks[6], (DEPTH, HD_A)),
        'lambda_q1': nrm(ks[7], (DEPTH, HD_A), 0.1),
        'lambda_k1': nrm(ks[8], (DEPTH, HD_A), 0.1),
        'lambda_q2': nrm(ks[9], (DEPTH, HD_A), 0.1),
        'lambda_k2': nrm(ks[10], (DEPTH, HD_A), 0.1),
        'subln_a': gain(ks[11], (DEPTH, 2 * HD_A)),
        'rel_bias': nrm(ks[12], (REL_BUCKETS, H_A), 0.1),
        'qk_norm_b_q': gain(ks[13], (DEPTH, HD_B)),
        'qk_norm_b_k': gain(ks[14], (DEPTH, HD_B)),
        'na_rpb': nrm(ks[15], (DEPTH, H_B, 2 * NA_MAX_ROWS - 1, 2 * NA_COLS - 1), 0.1),
        'w_br_a': nrm(ks[16], (DEPTH, W_A, D_MODEL), W_A ** -0.5),
        'w_br_b': nrm(ks[17], (DEPTH, W_B, D_MODEL), W_B ** -0.5),
        'w_out': nrm(ks[18], (DEPTH, D_MODEL, D_MODEL), D_MODEL ** -0.5),
        'norm_ffn': gain(ks[19], (DEPTH, D_MODEL)),
        'w_router': nrm(ks[20], (DEPTH, D_MODEL, N_EXPERTS), D_MODEL ** -0.5),
        'b_router': nrm(ks[21], (DEPTH, N_EXPERTS), 0.01),
        'w_gate': nrm(ks[22], (DEPTH, N_EXPERTS, D_MODEL, D_FF), D_MODEL ** -0.5),
        'b_gate': nrm(ks[23], (DEPTH, N_EXPERTS, D_FF), 0.01),
        'w_up': nrm(ks[24], (DEPTH, N_EXPERTS, D_MODEL, D_FF), D_MODEL ** -0.5),
        'b_up': nrm(ks[25], (DEPTH, N_EXPERTS, D_FF), 0.01),
        'w_down': nrm(ks[26], (DEPTH, N_EXPERTS, D_FF, D_MODEL), D_FF ** -0.5),
        'b_down': nrm(ks[27], (DEPTH, N_EXPERTS, D_MODEL), 0.01),
    }


def reference(x_prompt, x_sample, meta_tokens, norm_mix, w_in, qk_norm_a_q, qk_norm_a_k,
              lambda_q1, lambda_k1, lambda_q2, lambda_k2, subln_a, rel_bias, qk_norm_b_q,
              qk_norm_b_k, na_rpb, w_br_a, w_br_b, w_out, norm_ffn, w_router, b_router,
              w_gate, b_gate, w_up, b_up, w_down, b_down):
    p = {
        'meta_tokens': meta_tokens, 'norm_mix': norm_mix, 'w_in': w_in,
        'qk_norm_a_q': qk_norm_a_q, 'qk_norm_a_k': qk_norm_a_k,
        'lambda_q1': lambda_q1, 'lambda_k1': lambda_k1, 'lambda_q2': lambda_q2, 'lambda_k2': lambda_k2,
        'subln_a': subln_a, 'rel_bias': rel_bias,
        'qk_norm_b_q': qk_norm_b_q, 'qk_norm_b_k': qk_norm_b_k, 'na_rpb': na_rpb,
        'w_br_a': w_br_a, 'w_br_b': w_br_b, 'w_out': w_out, 'norm_ffn': norm_ffn,
        'w_router': w_router, 'b_router': b_router, 'w_gate': w_gate, 'b_gate': b_gate,
        'w_up': w_up, 'b_up': b_up, 'w_down': w_down, 'b_down': b_down,
    }
    y_prompt = _trunk(x_prompt, p)
    y_sample = _trunk(x_sample, p)
    return (y_prompt, y_sample)
```

```python
import functools
import math

import numpy as np
import jax
import jax.numpy as jnp
from jax import lax
from jax.experimental import pallas as pl
from jax.experimental.pallas import tpu as pltpu

F32 = jnp.float32
BF16 = jnp.bfloat16
U32 = jnp.uint32
I32 = jnp.int32

N_META = 16
GRID_W = 64
HD = 128
REL_BUCKETS = 32
REL_MAX_DIST = 128
NA_MAX_ROWS = 8
NA_COLS = 16
TOP_K = 4
SWIGLU_LIMIT = 7.0
SWIGLU_ALPHA = 1.702
EPS = 1e-6
LAMBDA_INIT = 0.8 - 0.6 * math.exp(-0.3 * 0)
NEG = -1e30

LANES = 128
VMEM_LIMIT_MB = 56
ROW_TILE = 1024
COL_TILE = 1024
NORM_TILE = 256
NA_ROWS = 8
NA_BLOCK = NA_ROWS * GRID_W
MOE_TILE = 512
MOE_FF_TILE = 256
ATTN_TQ = 1024
ATTN_TK = 1024
COMBINE_TILE = 128


def _params(semantics, vmem_mb=VMEM_LIMIT_MB):
    return pltpu.CompilerParams(dimension_semantics=semantics,
                                vmem_limit_bytes=vmem_mb * 2**20)


def _rmsnorm_kernel(x_ref, g_ref, o_ref):
    x = x_ref[...].astype(F32)
    ms = jnp.mean(x * x, axis=-1, keepdims=True)
    o_ref[...] = (x * lax.rsqrt(ms + EPS) * g_ref[...]).astype(o_ref.dtype)


def _rmsnorm_rows(x, g, out_dtype):
    m, d = x.shape
    tm = min(NORM_TILE, m)
    return pl.pallas_call(
        _rmsnorm_kernel,
        grid=(m // tm,),
        in_specs=[pl.BlockSpec((tm, d), lambda i: (i, 0)),
                  pl.BlockSpec((1, d), lambda i: (0, 0))],
        out_specs=pl.BlockSpec((tm, d), lambda i: (i, 0)),
        out_shape=jax.ShapeDtypeStruct((m, d), out_dtype),
        compiler_params=_params(("parallel",)),
        name="rmsnorm_rows",
    )(x, g.reshape(1, d).astype(F32))


def _inproj_kernel(x_ref, w_ref, gain_ref, o_ref, *, norm_ranges, gate_start):
    j = pl.program_id(0)
    acc = jnp.dot(x_ref[...], w_ref[...], preferred_element_type=F32)
    tn = acc.shape[1]

    is_norm = False
    for lo, hi in norm_ranges:
        is_norm = jnp.logical_or(is_norm, jnp.logical_and(j >= lo, j < hi))
    is_gate = j >= gate_start

    @pl.when(is_norm)
    def _():
        for c in range(tn // HD):
            blk = acc[:, c * HD:(c + 1) * HD]
            ms = jnp.mean(blk * blk, axis=-1, keepdims=True)
            o_ref[:, c * HD:(c + 1) * HD] = (
                blk * lax.rsqrt(ms + EPS) * gain_ref[:, c * HD:(c + 1) * HD]).astype(o_ref.dtype)

    @pl.when(is_gate)
    def _():
        o_ref[...] = (1.0 / (1.0 + jnp.exp(-acc))).astype(o_ref.dtype)

    @pl.when(jnp.logical_not(jnp.logical_or(is_norm, is_gate)))
    def _():
        o_ref[...] = acc.astype(o_ref.dtype)


def _inproj(hn, w, gain, w_a, w_b):
    m, d = hn.shape
    n = w.shape[1]
    tm = min(ROW_TILE, m)
    tn = min(COL_TILE, w_a, w_b)
    assert w_a % tn == 0 and w_b % tn == 0 and n % tn == 0 and m % tm == 0
    norm_ranges = ((0, 2 * w_a // tn), (3 * w_a // tn, (3 * w_a + 2 * w_b) // tn))
    gate_start = (3 * w_a + 3 * w_b) // tn
    kern = functools.partial(_inproj_kernel, norm_ranges=norm_ranges, gate_start=gate_start)
    return pl.pallas_call(
        kern,
        grid=(n // tn, m // tm),
        in_specs=[pl.BlockSpec((tm, d), lambda j, i: (i, 0)),
                  pl.BlockSpec((d, tn), lambda j, i: (0, j)),
                  pl.BlockSpec((1, tn), lambda j, i: (0, j))],
        out_specs=pl.BlockSpec((tm, tn), lambda j, i: (i, j)),
        out_shape=jax.ShapeDtypeStruct((m, n), BF16),
        compiler_params=_params(("parallel", "parallel")),
        name="inproj",
    )(hn, w, gain)


def _t5_bucket(rel):
    half = REL_BUCKETS // 2
    max_exact = half // 2
    sign = (rel > 0).astype(I32) * half
    n = jnp.abs(rel)
    nf = jnp.maximum(n, 1).astype(F32)
    large = max_exact + (jnp.log(nf / max_exact) / math.log(REL_MAX_DIST / max_exact)
                         * (half - max_exact)).astype(I32)
    large = jnp.minimum(large, half - 1)
    return sign + jnp.where(n < max_exact, n, large)


def _bias_tile_plan(qpos0, nq, tq, nk, tk):
    keys, ids = [], np.zeros((nq, nk), np.int32)
    for i in range(nq):
        for j in range(nk):
            d = (N_META + j * tk) - (qpos0 + i * tq)
            if d - (tq - 1) >= REL_MAX_DIST:
                key = ("far", REL_MAX_DIST)
            elif d + (tk - 1) <= -REL_MAX_DIST:
                key = ("far", -REL_MAX_DIST)
            else:
                key = ("near", d)
            if key not in keys:
                keys.append(key)
            ids[i, j] = keys.index(key)
    ramp = np.arange(tk)[None, :] - np.arange(tq)[:, None]
    rel = np.stack([np.full((tq, tk), v) if kind == "far" else v + ramp for kind, v in keys])
    return rel.astype(np.int32), ids.reshape(-1)


def _bias_tiles(rel_bias, rel):
    b = rel_bias[_t5_bucket(jnp.asarray(rel))]
    return jnp.transpose(b, (3, 0, 1, 2)).astype(F32)


def _meta_bias_tiles(rel_bias, qpos0, nq, tq):
    n = 1
    while n < nq and qpos0 + (n - 1) * tq - (N_META - 1) < REL_MAX_DIST:
        n += 1
    qpos = qpos0 + np.arange(n * tq).reshape(n, tq)
    rel = np.arange(N_META)[None, None, :] - qpos[:, :, None]
    b = jnp.transpose(rel_bias[_t5_bucket(jnp.asarray(rel.astype(np.int32)))], (3, 0, 1, 2)).astype(F32)
    pad = jnp.full(b.shape[:3] + (LANES - N_META,), NEG, F32)
    return jnp.concatenate([b, pad], axis=-1), n


def _softmax_step(s, m_ref, l_ref, a_ref, v, first):
    m_cur = jnp.max(s, axis=-1, keepdims=True)
    if first:
        m_new = m_cur
        p = jnp.exp(s - m_new)
        l_ref[...] = jnp.sum(p, axis=-1, keepdims=True)
        a_ref[...] = jnp.dot(p.astype(v.dtype), v, preferred_element_type=F32)
    else:
        m_prev = m_ref[...]
        m_new = jnp.maximum(m_prev, m_cur)
        alpha = jnp.exp(m_prev - m_new)
        p = jnp.exp(s - m_new)
        l_ref[...] = alpha * l_ref[...] + jnp.sum(p, axis=-1, keepdims=True)
        a_ref[...] = alpha * a_ref[...] + jnp.dot(p.astype(v.dtype), v, preferred_element_type=F32)
    m_ref[...] = m_new


def _qk(q, k):
    return lax.dot_general(q, k, (((1,), (1,)), ((), ())), preferred_element_type=F32)


def _pad_meta(x):
    return jnp.concatenate([x, jnp.zeros((LANES - N_META, x.shape[1]), x.dtype)], axis=0)


def _diff_attn_kernel(tid_ref, q_ref, k_ref, v_ref, km_ref, vm_ref, bias_ref, biasm_ref,
                      lq1_ref, lk1_ref, lq2_ref, lk2_ref, sub_ref, prev_ref, o_ref,
                      m1, l1, a1, m2, l2, a2):
    del tid_ref, prev_ref
    j = pl.program_id(3)
    nk = pl.num_programs(3)
    q = q_ref[...]
    q1, q2 = q[:, :HD], q[:, HD:]

    @pl.when(j == 0)
    def _():
        km = _pad_meta(km_ref[...])
        vm = _pad_meta(vm_ref[...])
        bm = biasm_ref[0, 0]
        _softmax_step(_qk(q1, km[:, :HD]) + bm, m1, l1, a1, vm, True)
        _softmax_step(_qk(q2, km[:, HD:]) + bm, m2, l2, a2, vm, True)

    k = k_ref[...]
    v = v_ref[...]
    b = bias_ref[0, 0]
    _softmax_step(_qk(q1, k[:, :HD]) + b, m1, l1, a1, v, False)
    _softmax_step(_qk(q2, k[:, HD:]) + b, m2, l2, a2, v, False)

    @pl.when(j == nk - 1)
    def _():
        lam = (jnp.exp(jnp.sum(lq1_ref[...] * lk1_ref[...], axis=-1, keepdims=True))
               - jnp.exp(jnp.sum(lq2_ref[...] * lk2_ref[...], axis=-1, keepdims=True))
               + LAMBDA_INIT)
        o = a1[...] / l1[...] - lam * (a2[...] / l2[...])
        ms = jnp.mean(o * o, axis=-1, keepdims=True)
        o_ref[...] = (o * lax.rsqrt(ms + EPS) * sub_ref[...] * (1.0 - LAMBDA_INIT)).astype(o_ref.dtype)


def _diff_attn(proj, oa, rel_bias, lam_vecs, subln, *, w_a, n_seq, seq_len, kv_row0,
               q_row0, q_stride, tq, nq, qpos0, meta_row0, tk):
    h_a = w_a // (2 * HD)
    nk = seq_len // tk
    assert seq_len % tk == 0 and kv_row0 % tk == 0 and q_row0 % tq == 0 and q_stride % tq == 0
    assert meta_row0 % N_META == 0
    rel, ids = _bias_tile_plan(qpos0, nq, tq, nk, tk)
    tiles = _bias_tiles(rel_bias, rel)
    biasm, n_bm = _meta_bias_tiles(rel_bias, qpos0, nq, tq)
    qb0, qbs = q_row0 // tq, q_stride // tq
    kb0, kbs = kv_row0 // tk, seq_len // tk
    mb0 = meta_row0 // N_META
    kcol, vcol = w_a // (2 * HD), 2 * w_a // (2 * HD)
    dv = 2 * HD

    def qmap(b, h, i, j, t):
        return (qb0 + b * qbs + i, h)

    vec = pl.BlockSpec((1, HD), lambda b, h, i, j, t: (0, 0))
    grid_spec = pltpu.PrefetchScalarGridSpec(
        num_scalar_prefetch=1,
        grid=(n_seq, h_a, nq, nk),
        in_specs=[
            pl.BlockSpec((tq, dv), qmap),
            pl.BlockSpec((tk, dv), lambda b, h, i, j, t: (kb0 + b * kbs + j, kcol + h)),
            pl.BlockSpec((tk, dv), lambda b, h, i, j, t: (kb0 + b * kbs + j, vcol + h)),
            pl.BlockSpec((N_META, dv), lambda b, h, i, j, t: (mb0 + b, kcol + h)),
            pl.BlockSpec((N_META, dv), lambda b, h, i, j, t: (mb0 + b, vcol + h)),
            pl.BlockSpec((1, 1, tq, tk), lambda b, h, i, j, t: (h, t[i * nk + j], 0, 0)),
            pl.BlockSpec((1, 1, tq, LANES), lambda b, h, i, j, t: (h, jnp.minimum(i, n_bm - 1), 0, 0)),
            vec, vec, vec, vec,
            pl.BlockSpec((1, dv), lambda b, h, i, j, t: (0, 0)),
            pl.BlockSpec(memory_space=pl.ANY),
        ],
        out_specs=pl.BlockSpec((tq, dv), qmap),
        scratch_shapes=[pltpu.VMEM((tq, 1), F32), pltpu.VMEM((tq, 1), F32), pltpu.VMEM((tq, dv), F32),
                        pltpu.VMEM((tq, 1), F32), pltpu.VMEM((tq, 1), F32), pltpu.VMEM((tq, dv), F32)],
    )
    return pl.pallas_call(
        _diff_attn_kernel,
        grid_spec=grid_spec,
        out_shape=jax.ShapeDtypeStruct(oa.shape, oa.dtype),
        input_output_aliases={13: 0},
        compiler_params=_params(("parallel", "parallel", "parallel", "arbitrary")),
        name=f"diff_attn_q{tq}",
    )(jnp.asarray(ids), proj, proj, proj, proj, proj, tiles, biasm, *lam_vecs, subln, oa)


def _na_plan():
    rows_c = 3 * NA_ROWS
    a = np.arange(NA_ROWS)
    kl = np.arange(3 * NA_ROWS)
    ridx = np.zeros((3, NA_ROWS, 3 * NA_ROWS), np.int32)
    rvalid = np.zeros((3, NA_ROWS, 3 * NA_ROWS), bool)
    for v in range(3):
        r = v * NA_ROWS + a
        rs = np.clip(r - NA_MAX_ROWS // 2, 0, rows_c - NA_MAX_ROWS)
        kr = (v - 1) * NA_ROWS + kl
        off = kr[None, :] - r[:, None]
        ok = (kr[None, :] >= rs[:, None]) & (kr[None, :] < rs[:, None] + NA_MAX_ROWS)
        ok &= (kr[None, :] >= 0) & (kr[None, :] < rows_c)
        ridx[v] = np.clip(off + NA_MAX_ROWS - 1, 0, 2 * NA_MAX_ROWS - 2)
        rvalid[v] = ok
    c = np.arange(GRID_W)
    cs = np.clip(c - NA_COLS // 2, 0, GRID_W - NA_COLS)
    coff = c[None, :] - c[:, None]
    cvalid = (c[None, :] >= cs[:, None]) & (c[None, :] < cs[:, None] + NA_COLS)
    cidx = np.clip(coff + NA_COLS - 1, 0, 2 * NA_COLS - 2).astype(np.int32)
    return ridx, rvalid, cidx, cvalid


def _na_bias_tiles(rpb):
    ridx, rvalid, cidx, cvalid = _na_plan()
    h = rpb.shape[0]
    t1 = rpb[:, :, cidx]
    t2 = t1[:, ridx]
    valid = rvalid[:, :, :, None, None] & cvalid[None, None, None]
    t2 = jnp.where(jnp.asarray(valid)[None], t2, NEG)
    t2 = jnp.transpose(t2, (0, 1, 2, 4, 3, 5))
    return t2.reshape(h, 3, NA_BLOCK, 3 * NA_BLOCK).astype(F32)


def _na_kernel(q_ref, kp_ref, kc_ref, kn_ref, vp_ref, vc_ref, vn_ref, km_ref, vm_ref,
               bias_ref, prev_ref, o_ref):
    del prev_ref
    q = q_ref[...]
    k = jnp.concatenate([kp_ref[...], kc_ref[...], kn_ref[...]], axis=0)
    v = jnp.concatenate([vp_ref[...], vc_ref[...], vn_ref[...]], axis=0)
    s = _qk(q, k) + bias_ref[0, 0]
    km = _pad_meta(km_ref[...])
    vm = _pad_meta(vm_ref[...])
    lane = lax.broadcasted_iota(I32, (q.shape[0], LANES), 1)
    sm = jnp.where(lane < N_META, _qk(q, km), NEG)
    m = jnp.maximum(jnp.max(s, axis=-1, keepdims=True), jnp.max(sm, axis=-1, keepdims=True))
    p = jnp.exp(s - m)
    pm = jnp.exp(sm - m)
    l = jnp.sum(p, axis=-1, keepdims=True) + jnp.sum(pm, axis=-1, keepdims=True)
    o = (jnp.dot(p.astype(v.dtype), v, preferred_element_type=F32)
         + jnp.dot(pm.astype(vm.dtype), vm, preferred_element_type=F32))
    o_ref[...] = (o / l).astype(o_ref.dtype)


def _na_attn(proj, ob, bias, *, w_a, w_b, n_seq, seq_len, row0, meta_row0):
    h_b = w_b // HD
    nblk = seq_len // NA_BLOCK
    assert seq_len % NA_BLOCK == 0 and nblk >= 2 and row0 % NA_BLOCK == 0
    rb0 = row0 // NA_BLOCK
    mb0 = meta_row0 // N_META
    qcol, kcol, vcol = 3 * w_a // HD, (3 * w_a + w_b) // HD, (3 * w_a + 2 * w_b) // HD

    def blk(col, shift):
        def index(h, b, i):
            return (rb0 + b * nblk + jnp.clip(i + shift, 0, nblk - 1), col + h)
        return pl.BlockSpec((NA_BLOCK, HD), index)

    def variant(i):
        return jnp.where(i == 0, 0, jnp.where(i == nblk - 1, 2, 1))

    return pl.pallas_call(
        _na_kernel,
        grid=(h_b, n_seq, nblk),
        in_specs=[blk(qcol, 0), blk(kcol, -1), blk(kcol, 0), blk(kcol, 1),
                  blk(vcol, -1), blk(vcol, 0), blk(vcol, 1),
                  pl.BlockSpec((N_META, HD), lambda h, b, i: (mb0 + b, kcol + h)),
                  pl.BlockSpec((N_META, HD), lambda h, b, i: (mb0 + b, vcol + h)),
                  pl.BlockSpec((1, 1, NA_BLOCK, 3 * NA_BLOCK), lambda h, b, i: (h, variant(i), 0, 0)),
                  pl.BlockSpec(memory_space=pl.ANY)],
        out_specs=pl.BlockSpec((NA_BLOCK, HD), lambda h, b, i: (rb0 + b * nblk + i, h)),
        out_shape=jax.ShapeDtypeStruct(ob.shape, ob.dtype),
        input_output_aliases={10: 0},
        compiler_params=_params(("parallel", "parallel", "parallel")),
        name="na_attn",
    )(proj, proj, proj, proj, proj, proj, proj, proj, proj, bias, ob)


def _na_meta_kernel(q_ref, km_ref, vm_ref, prev_ref, o_ref):
    del prev_ref
    q = jnp.concatenate([q_ref[...], jnp.zeros((LANES - N_META, HD), q_ref.dtype)], axis=0)
    km = _pad_meta(km_ref[...])
    vm = _pad_meta(vm_ref[...])
    lane = lax.broadcasted_iota(I32, (LANES, LANES), 1)
    s = jnp.where(lane < N_META, _qk(q, km), NEG)
    m = jnp.max(s, axis=-1, keepdims=True)
    p = jnp.exp(s - m)
    o = jnp.dot(p.astype(vm.dtype), vm, preferred_element_type=F32) / jnp.sum(p, axis=-1, keepdims=True)
    o_ref[...] = o[:N_META].astype(o_ref.dtype)


def _na_meta_attn(proj, ob, *, w_a, w_b, n_seq, meta_row0):
    h_b = w_b // HD
    mb0 = meta_row0 // N_META
    qcol, kcol, vcol = 3 * w_a // HD, (3 * w_a + w_b) // HD, (3 * w_a + 2 * w_b) // HD
    return pl.pallas_call(
        _na_meta_kernel,
        grid=(n_seq, h_b),
        in_specs=[pl.BlockSpec((N_META, HD), lambda s, h: (mb0 + s, qcol + h)),
                  pl.BlockSpec((N_META, HD), lambda s, h: (mb0 + s, kcol + h)),
                  pl.BlockSpec((N_META, HD), lambda s, h: (mb0 + s, vcol + h)),
                  pl.BlockSpec(memory_space=pl.ANY)],
        out_specs=pl.BlockSpec((N_META, HD), lambda s, h: (mb0 + s, h)),
        out_shape=jax.ShapeDtypeStruct(ob.shape, ob.dtype),
        input_output_aliases={3: 0},
        compiler_params=_params(("parallel", "parallel")),
        name="na_meta_attn",
    )(proj, proj, proj, ob)


def _merge_kernel(oa_ref, ob_ref, wa_ref, wb_ref, ga_ref, gb_ref, o_ref):
    a = jnp.dot(oa_ref[...], wa_ref[...], preferred_element_type=F32)
    b = jnp.dot(ob_ref[...], wb_ref[...], preferred_element_type=F32)
    o_ref[...] = (ga_ref[...].astype(F32) * a + gb_ref[...].astype(F32) * b).astype(o_ref.dtype)


def _merge(oa, ob, wa, wb, proj, gate_col0):
    m, w_a = oa.shape
    w_b = ob.shape[1]
    d = wa.shape[1]
    tm = min(ROW_TILE, m)
    tn = min(COL_TILE, d)
    ga0 = gate_col0 // tn
    gb0 = (gate_col0 + d) // tn
    return pl.pallas_call(
        _merge_kernel,
        grid=(d // tn, m // tm),
        in_specs=[pl.BlockSpec((tm, w_a), lambda j, i: (i, 0)),
                  pl.BlockSpec((tm, w_b), lambda j, i: (i, 0)),
                  pl.BlockSpec((w_a, tn), lambda j, i: (0, j)),
                  pl.BlockSpec((w_b, tn), lambda j, i: (0, j)),
                  pl.BlockSpec((tm, tn), lambda j, i: (i, ga0 + j)),
                  pl.BlockSpec((tm, tn), lambda j, i: (i, gb0 + j))],
        out_specs=pl.BlockSpec((tm, tn), lambda j, i: (i, j)),
        out_shape=jax.ShapeDtypeStruct((m, d), BF16),
        compiler_params=_params(("parallel", "parallel")),
        name="merge",
    )(oa, ob, wa, wb, proj, proj)


def _outproj_kernel(x_ref, w_ref, h_ref, o_ref):
    o_ref[...] = h_ref[...] + jnp.dot(x_ref[...], w_ref[...], preferred_element_type=F32)


def _outproj(merged, w, h):
    m, d = merged.shape
    n = w.shape[1]
    tm = min(ROW_TILE, m)
    tn = min(COL_TILE, n)
    return pl.pallas_call(
        _outproj_kernel,
        grid=(n // tn, m // tm),
        in_specs=[pl.BlockSpec((tm, d), lambda j, i: (i, 0)),
                  pl.BlockSpec((d, tn), lambda j, i: (0, j)),
                  pl.BlockSpec((tm, tn), lambda j, i: (i, j))],
        out_specs=pl.BlockSpec((tm, tn), lambda j, i: (i, j)),
        out_shape=jax.ShapeDtypeStruct((m, n), F32),
        compiler_params=_params(("parallel", "parallel")),
        name="outproj",
    )(merged, w, h)


def _pack_bf16_pairs(x):
    half = x.shape[1] // 2
    lo = lax.bitcast_convert_type(x[:, :half].astype(jnp.bfloat16).astype(F32), U32)
    hi = lax.bitcast_convert_type(x[:, half:].astype(jnp.bfloat16).astype(F32), U32)
    return (lo >> 16) | (hi & jnp.uint32(0xFFFF0000))


def _unpack_bf16_pairs(u):
    lo = lax.bitcast_convert_type(u << 16, F32).astype(BF16)
    hi = lax.bitcast_convert_type(u & jnp.uint32(0xFFFF0000), F32).astype(BF16)
    return jnp.concatenate([lo, hi], axis=1)


def _router_kernel(h_ref, g_ref, wr_ref, br_ref, hf_ref, ti_ref, tw_ref):
    x = h_ref[...]
    ms = jnp.mean(x * x, axis=-1, keepdims=True)
    hf = x * lax.rsqrt(ms + EPS) * g_ref[...]
    hf_ref[...] = _pack_bf16_pairs(hf)
    logits = jnp.dot(hf, wr_ref[...], preferred_element_type=F32,
                     precision=lax.Precision.HIGHEST) + br_ref[...]
    lane = lax.broadcasted_iota(I32, logits.shape, 1).astype(F32)
    vals, idxs = [], []
    for _ in range(TOP_K):
        mx = jnp.max(logits, axis=-1, keepdims=True)
        ix = jnp.min(jnp.where(logits == mx, lane, float(LANES)), axis=-1, keepdims=True)
        vals.append(mx)
        idxs.append(ix)
        logits = jnp.where(lane == ix, -jnp.inf, logits)
    es = [jnp.exp(v - vals[0]) for v in vals]
    tot = es[0]
    for e in es[1:]:
        tot = tot + e
    ti = jnp.zeros(lane.shape, F32)
    tw = jnp.zeros(lane.shape, F32)
    for k in range(TOP_K):
        ti = jnp.where(lane == k, idxs[k], ti)
        tw = jnp.where(lane == k, es[k] / tot, tw)
    ti_ref[...] = ti.astype(I32)
    tw_ref[...] = tw


def _router(h1, g, w_router, b_router):
    m, d = h1.shape
    e = w_router.shape[1]
    assert e <= LANES
    tm = min(NORM_TILE, m)
    wr = jnp.zeros((d, LANES), F32).at[:, :e].set(w_router.astype(F32))
    br = jnp.full((1, LANES), NEG, F32).at[0, :e].set(b_router.astype(F32))
    return pl.pallas_call(
        _router_kernel,
        grid=(m // tm,),
        in_specs=[pl.BlockSpec((tm, d), lambda i: (i, 0)),
                  pl.BlockSpec((1, d), lambda i: (0, 0)),
                  pl.BlockSpec((d, LANES), lambda i: (0, 0)),
                  pl.BlockSpec((1, LANES), lambda i: (0, 0))],
        out_specs=[pl.BlockSpec((tm, d // 2), lambda i: (i, 0)),
                   pl.BlockSpec((tm, LANES), lambda i: (i, 0)),
                   pl.BlockSpec((tm, LANES), lambda i: (i, 0))],
        out_shape=[jax.ShapeDtypeStruct((m, d // 2), U32),
                   jax.ShapeDtypeStruct((m, LANES), I32),
                   jax.ShapeDtypeStruct((m, LANES), F32)],
        compiler_params=_params(("parallel",)),
        name="router",
    )(h1, g.reshape(1, d).astype(F32), wr, br)


def _row_copy(src_ref, src_row, dst_ref, dst_row, sem):
    return pltpu.make_async_copy(src_ref.at[pl.ds(src_row, 1)], dst_ref.at[pl.ds(dst_row, 1)], sem)


def _gather_kernel(idx_ref, src_ref, o_ref, sem):
    rows = o_ref.shape[0]

    def issue(r, c):
        _row_copy(src_ref, idx_ref[0, r], o_ref, r, sem).start()
        return c

    def drain(r, c):
        _row_copy(src_ref, 0, o_ref, r, sem).wait()
        return c

    lax.fori_loop(0, rows, issue, 0)
    lax.fori_loop(0, rows, drain, 0)


def _gather_rows(src, slot_rows, tile):
    n_slots = slot_rows.shape[0]
    width = src.shape[1]
    nb = n_slots // tile
    return pl.pallas_call(
        _gather_kernel,
        grid=(nb,),
        in_specs=[pl.BlockSpec((None, 1, tile), lambda b: (b, 0, 0), memory_space=pltpu.SMEM),
                  pl.BlockSpec(memory_space=pl.ANY)],
        out_specs=pl.BlockSpec((tile, width), lambda b: (b, 0)),
        out_shape=jax.ShapeDtypeStruct((n_slots, width), src.dtype),
        scratch_shapes=[pltpu.SemaphoreType.DMA(())],
        compiler_params=_params(("arbitrary",)),
        name="moe_gather",
    )(slot_rows.reshape(nb, 1, tile), src)


def _moe_kernel(be_ref, nu_ref, x_ref, wg_ref, wu_ref, wd_ref, bg_ref, bu_ref, bd_ref, o_ref, xb):
    del be_ref
    b = pl.program_id(0)
    f = pl.program_id(1)

    @pl.when(b < nu_ref[0])
    def _():
        @pl.when(f == 0)
        def _():
            xb[...] = _unpack_bf16_pairs(x_ref[...])
            o_ref[...] = jnp.broadcast_to(bd_ref[0], o_ref.shape)

        x = xb[...]
        g = jnp.dot(x, wg_ref[0], preferred_element_type=F32) + bg_ref[0]
        u = jnp.dot(x, wu_ref[0], preferred_element_type=F32) + bu_ref[0]
        g = jnp.minimum(g, SWIGLU_LIMIT)
        u = jnp.clip(u, -SWIGLU_LIMIT, SWIGLU_LIMIT)
        act = (u + 1.0) * (g * (1.0 / (1.0 + jnp.exp(-SWIGLU_ALPHA * g))))
        o_ref[...] += jnp.dot(act.astype(BF16), wd_ref[0], preferred_element_type=F32)

    @pl.when(jnp.logical_and(b >= nu_ref[0], f == 0))
    def _():
        o_ref[...] = jnp.zeros(o_ref.shape, o_ref.dtype)


def _moe_experts(xs, block_e, n_used, wg, wu, wd, bg, bu, bd, tile):
    n_slots, half = xs.shape
    d = 2 * half
    e, _, ff = wg.shape
    tf = min(MOE_FF_TILE, ff)
    nb, nf = n_slots // tile, ff // tf

    def bb(b, nu):
        return jnp.minimum(b, nu[0] - 1)

    def fidx(b, f, nu):
        return jnp.where(b < nu[0], f, nf - 1)

    grid_spec = pltpu.PrefetchScalarGridSpec(
        num_scalar_prefetch=2,
        grid=(nb, nf),
        in_specs=[
            pl.BlockSpec((tile, half), lambda b, f, be, nu: (bb(b, nu), 0)),
            pl.BlockSpec((1, d, tf), lambda b, f, be, nu: (be[bb(b, nu)], 0, fidx(b, f, nu))),
            pl.BlockSpec((1, d, tf), lambda b, f, be, nu: (be[bb(b, nu)], 0, fidx(b, f, nu))),
            pl.BlockSpec((1, tf, d), lambda b, f, be, nu: (be[bb(b, nu)], fidx(b, f, nu), 0)),
            pl.BlockSpec((1, 1, tf), lambda b, f, be, nu: (be[bb(b, nu)], 0, fidx(b, f, nu))),
            pl.BlockSpec((1, 1, tf), lambda b, f, be, nu: (be[bb(b, nu)], 0, fidx(b, f, nu))),
            pl.BlockSpec((1, 1, d), lambda b, f, be, nu: (be[bb(b, nu)], 0, 0)),
        ],
        out_specs=pl.BlockSpec((tile, d), lambda b, f, be, nu: (b, 0)),
        scratch_shapes=[pltpu.VMEM((tile, d), BF16)],
    )
    return pl.pallas_call(
        _moe_kernel,
        grid_spec=grid_spec,
        out_shape=jax.ShapeDtypeStruct((n_slots, d), F32),
        compiler_params=_params(("arbitrary", "arbitrary")),
        name="moe_experts",
    )(block_e, n_used, xs, wg, wu, wd, bg.reshape(e, 1, ff), bu.reshape(e, 1, ff), bd.reshape(e, 1, d))


def _combine_kernel(pos_ref, w_ref, h_ref, ys_ref, o_ref, buf, sem):
    rows = o_ref.shape[0]

    def issue(r, c):
        for k in range(TOP_K):
            _row_copy(ys_ref, pos_ref[0, TOP_K * r + k], buf.at[k], r, sem).start()
        return c

    def drain(r, c):
        for k in range(TOP_K):
            _row_copy(ys_ref, 0, buf.at[k], r, sem).wait()
        return c

    lax.fori_loop(0, rows, issue, 0)
    lax.fori_loop(0, rows, drain, 0)
    w = w_ref[...]
    acc = h_ref[...]
    for k in range(TOP_K):
        acc = acc + w[:, k:k + 1] * buf[k]
    o_ref[...] = acc


def _combine(h1, tw, pos, ys, row0, n_rows):
    d = h1.shape[1]
    tc = min(COMBINE_TILE, n_rows)
    assert n_rows % tc == 0 and row0 % tc == 0
    nb, b0 = n_rows // tc, row0 // tc
    pos_blocks = pos[row0:row0 + n_rows].reshape(nb, 1, tc * TOP_K)
    return pl.pallas_call(
        _combine_kernel,
        grid=(nb,),
        in_specs=[pl.BlockSpec((None, 1, tc * TOP_K), lambda i: (i, 0, 0), memory_space=pltpu.SMEM),
                  pl.BlockSpec((tc, LANES), lambda i: (b0 + i, 0)),
                  pl.BlockSpec((tc, d), lambda i: (b0 + i, 0)),
                  pl.BlockSpec(memory_space=pl.ANY)],
        out_specs=pl.BlockSpec((tc, d), lambda i: (i, 0)),
        out_shape=jax.ShapeDtypeStruct((n_rows, d), F32),
        scratch_shapes=[pltpu.VMEM((TOP_K, tc, d), F32), pltpu.SemaphoreType.DMA(())],
        compiler_params=_params(("arbitrary",)),
        name="moe_combine",
    )(pos_blocks, tw, h1, ys)


def _dispatch_plan(top_i, n_experts, tile):
    n_tok = top_i.shape[0]
    nk = n_tok * TOP_K
    flat_e = top_i.reshape(-1)
    onehot = (flat_e[:, None] == jnp.arange(n_experts, dtype=I32)[None, :]).astype(I32)
    rank = jnp.take_along_axis(jnp.cumsum(onehot, axis=0), flat_e[:, None], axis=1)[:, 0] - 1
    counts = jnp.sum(onehot, axis=0)
    nblk = (counts + tile - 1) // tile
    blk_end = jnp.cumsum(nblk)
    blk_start = blk_end - nblk
    pos = blk_start[flat_e] * tile + rank
    nb_max = -(-nk // tile) + n_experts
    block_e = jnp.clip(jnp.searchsorted(blk_end, jnp.arange(nb_max, dtype=I32), side="right"),
                       0, n_experts - 1).astype(I32)
    n_used = blk_end[-1:].astype(I32)
    return pos.astype(I32), block_e, n_used, nb_max


def _round_up(x, m):
    return -(-x // m) * m


def kernel(x_prompt, x_sample, meta_tokens, norm_mix, w_in, qk_norm_a_q, qk_norm_a_k, lambda_q1, lambda_k1, lambda_q2, lambda_k2, subln_a, rel_bias, qk_norm_b_q, qk_norm_b_k, na_rpb, w_br_a, w_br_b, w_out, norm_ffn, w_router, b_router, w_gate, b_gate, w_up, b_up, w_down, b_down):
    d = x_prompt.shape[-1]
    w_a = w_br_a.shape[1]
    w_b = w_br_b.shape[1]
    n_experts = w_router.shape[-1]
    groups = [x_prompt, x_sample]

    row0s, seqs = [], []
    m_real = 0
    for x in groups:
        row0s.append(m_real)
        m_real += x.shape[0] * x.shape[1]
    n_seq = sum(x.shape[0] for x in groups)
    meta_row0 = m_real
    m_tok = m_real + n_seq * N_META
    m_pad = _round_up(m_tok + 1, ROW_TILE)
    h_all = jnp.concatenate(
        [x.reshape(-1, d) for x in groups]
        + [jnp.tile(meta_tokens.astype(F32), (n_seq, 1)), jnp.zeros((m_pad - m_tok, d), F32)], axis=0)

    hn = _rmsnorm_rows(h_all, norm_mix[0], BF16)
    scale_a, scale_b = HD ** -0.5, HD ** -0.5
    ones = lambda n: jnp.ones((n,), F32)
    gain = jnp.concatenate([
        jnp.tile(qk_norm_a_q[0].astype(F32) * scale_a, w_a // HD), jnp.tile(qk_norm_a_k[0].astype(F32), w_a // HD),
        ones(w_a),
        jnp.tile(qk_norm_b_q[0].astype(F32) * scale_b, w_b // HD), jnp.tile(qk_norm_b_k[0].astype(F32), w_b // HD),
        ones(w_b + 2 * d)]).reshape(1, -1)
    proj = _inproj(hn, w_in[0].astype(BF16), gain, w_a, w_b)

    lam_vecs = [v[0].reshape(1, HD).astype(F32) for v in (lambda_q1, lambda_k1, lambda_q2, lambda_k2)]
    subln = subln_a[0].reshape(1, 2 * HD).astype(F32)
    na_bias = _na_bias_tiles(na_rpb[0].astype(F32))
    oa = jnp.zeros((m_pad, w_a), BF16)
    ob = jnp.zeros((m_pad, w_b), BF16)
    seq0 = 0
    for x, row0 in zip(groups, row0s):
        nb, s = x.shape[0], x.shape[1]
        mrow = meta_row0 + seq0 * N_META
        tq = min(ATTN_TQ, s)
        common = dict(w_a=w_a, n_seq=nb, seq_len=s, kv_row0=row0, meta_row0=mrow, tk=min(ATTN_TK, s))
        oa = _diff_attn(proj, oa, rel_bias.astype(F32), lam_vecs, subln, q_row0=row0, q_stride=s,
                        tq=tq, nq=s // tq, qpos0=N_META, **common)
        oa = _diff_attn(proj, oa, rel_bias.astype(F32), lam_vecs, subln, q_row0=mrow, q_stride=N_META,
                        tq=N_META, nq=1, qpos0=0, **common)
        ob = _na_attn(proj, ob, na_bias, w_a=w_a, w_b=w_b, n_seq=nb, seq_len=s, row0=row0, meta_row0=mrow)
        seq0 += nb
    ob = _na_meta_attn(proj, ob, w_a=w_a, w_b=w_b, n_seq=n_seq, meta_row0=meta_row0)

    merged = _merge(oa, ob, w_br_a[0].astype(BF16), w_br_b[0].astype(BF16), proj, 3 * w_a + 3 * w_b)
    h1 = _outproj(merged, w_out[0].astype(BF16), h_all)

    hf, top_i, top_w = _router(h1, norm_ffn[0], w_router[0], b_router[0])
    pos, block_e, n_used, nb_max = _dispatch_plan(top_i[:m_tok, :TOP_K], n_experts, MOE_TILE)
    slot_rows = jnp.full((nb_max * MOE_TILE,), m_tok, I32).at[pos].set(
        jnp.repeat(jnp.arange(m_tok, dtype=I32), TOP_K))
    xs = _gather_rows(hf, slot_rows, MOE_TILE)
    ys = _moe_experts(xs, block_e, n_used, w_gate[0].astype(BF16), w_up[0].astype(BF16),
                      w_down[0].astype(BF16), b_gate[0].astype(F32), b_up[0].astype(F32),
                      b_down[0].astype(F32), MOE_TILE)
    pos2 = pos.reshape(m_tok, TOP_K)
    outs = []
    for x, row0 in zip(groups, row0s):
        n_rows = x.shape[0] * x.shape[1]
        outs.append(_combine(h1, top_w, pos2, ys, row0, n_rows).reshape(x.shape))
    return tuple(outs)
```

```python
import functools
import math

import numpy as np
import jax
import jax.numpy as jnp
from jax import lax
from jax.experimental import pallas as pl
from jax.experimental.pallas import tpu as pltpu

F32 = jnp.float32
BF16 = jnp.bfloat16
U32 = jnp.uint32
I32 = jnp.int32

N_META = 16
GRID_W = 64
HD = 128
REL_BUCKETS = 32
REL_MAX_DIST = 128
NA_MAX_ROWS = 8
NA_COLS = 16
TOP_K = 4
SWIGLU_LIMIT = 7.0
SWIGLU_ALPHA = 1.702
EPS = 1e-6
LAMBDA_INIT = 0.8 - 0.6 * math.exp(-0.3 * 0)
NEG = -1e30

LANES = 128
VMEM_LIMIT_MB = 56
ROW_TILE = 1024
COL_TILE = 1024
NORM_TILE = 256
NA_ROWS = 8
NA_BLOCK = NA_ROWS * GRID_W
NA_WIN_ROWS = NA_ROWS + NA_MAX_ROWS
NA_WIN = NA_WIN_ROWS * GRID_W
NA_KBLOCK = NA_WIN // 4
MOE_TILE = 512
MOE_FF_TILE = 256
ATTN_TQ = 1024
ATTN_TK = 1024
COMBINE_TILE = 128
DMA_UNROLL = 8


def _params(semantics, vmem_mb=VMEM_LIMIT_MB):
    return pltpu.CompilerParams(dimension_semantics=semantics,
                                vmem_limit_bytes=vmem_mb * 2**20)


def _rmsnorm_kernel(x_ref, g_ref, o_ref):
    x = x_ref[...].astype(F32)
    ms = jnp.mean(x * x, axis=-1, keepdims=True)
    o_ref[...] = (x * lax.rsqrt(ms + EPS) * g_ref[...]).astype(o_ref.dtype)


def _rmsnorm_rows(x, g, out_dtype):
    m, d = x.shape
    tm = min(NORM_TILE, m)
    return pl.pallas_call(
        _rmsnorm_kernel,
        grid=(m // tm,),
        in_specs=[pl.BlockSpec((tm, d), lambda i: (i, 0)),
                  pl.BlockSpec((1, d), lambda i: (0, 0))],
        out_specs=pl.BlockSpec((tm, d), lambda i: (i, 0)),
        out_shape=jax.ShapeDtypeStruct((m, d), out_dtype),
        compiler_params=_params(("parallel",)),
        name="rmsnorm_rows",
    )(x, g.reshape(1, d).astype(F32))


def _inproj_kernel(x_ref, w_ref, gain_ref, o_ref, *, norm_ranges, gate_start):
    j = pl.program_id(0)
    acc = jnp.dot(x_ref[...], w_ref[...], preferred_element_type=F32)
    tn = acc.shape[1]

    is_norm = False
    for lo, hi in norm_ranges:
        is_norm = jnp.logical_or(is_norm, jnp.logical_and(j >= lo, j < hi))
    is_gate = j >= gate_start

    @pl.when(is_norm)
    def _():
        for c in range(tn // HD):
            blk = acc[:, c * HD:(c + 1) * HD]
            ms = jnp.mean(blk * blk, axis=-1, keepdims=True)
            o_ref[:, c * HD:(c + 1) * HD] = (
                blk * lax.rsqrt(ms + EPS) * gain_ref[:, c * HD:(c + 1) * HD]).astype(o_ref.dtype)

    @pl.when(is_gate)
    def _():
        o_ref[...] = (1.0 / (1.0 + jnp.exp(-acc))).astype(o_ref.dtype)

    @pl.when(jnp.logical_not(jnp.logical_or(is_norm, is_gate)))
    def _():
        o_ref[...] = acc.astype(o_ref.dtype)


def _inproj(hn, w, gain, w_a, w_b):
    m, d = hn.shape
    n = w.shape[1]
    tm = min(ROW_TILE, m)
    tn = min(COL_TILE, w_a, w_b)
    assert w_a % tn == 0 and w_b % tn == 0 and n % tn == 0 and m % tm == 0
    norm_ranges = ((0, 2 * w_a // tn), (3 * w_a // tn, (3 * w_a + 2 * w_b) // tn))
    gate_start = (3 * w_a + 3 * w_b) // tn
    kern = functools.partial(_inproj_kernel, norm_ranges=norm_ranges, gate_start=gate_start)
    return pl.pallas_call(
        kern,
        grid=(n // tn, m // tm),
        in_specs=[pl.BlockSpec((tm, d), lambda j, i: (i, 0)),
                  pl.BlockSpec((d, tn), lambda j, i: (0, j)),
                  pl.BlockSpec((1, tn), lambda j, i: (0, j))],
        out_specs=pl.BlockSpec((tm, tn), lambda j, i: (i, j)),
        out_shape=jax.ShapeDtypeStruct((m, n), BF16),
        compiler_params=_params(("parallel", "parallel")),
        name="inproj",
    )(hn, w, gain)


def _t5_bucket(rel):
    half = REL_BUCKETS // 2
    max_exact = half // 2
    sign = (rel > 0).astype(I32) * half
    n = jnp.abs(rel)
    nf = jnp.maximum(n, 1).astype(F32)
    large = max_exact + (jnp.log(nf / max_exact) / math.log(REL_MAX_DIST / max_exact)
                         * (half - max_exact)).astype(I32)
    large = jnp.minimum(large, half - 1)
    return sign + jnp.where(n < max_exact, n, large)


def _bias_tile_plan(qpos0, nq, tq, nk, tk):
    keys, ids = [], np.zeros((nq, nk), np.int32)
    for i in range(nq):
        for j in range(nk):
            d = (N_META + j * tk) - (qpos0 + i * tq)
            if d - (tq - 1) >= REL_MAX_DIST:
                key = ("far", REL_MAX_DIST)
            elif d + (tk - 1) <= -REL_MAX_DIST:
                key = ("far", -REL_MAX_DIST)
            else:
                key = ("near", d)
            if key not in keys:
                keys.append(key)
            ids[i, j] = keys.index(key)
    return tuple(keys), ids.reshape(-1)


def _bias_tiles(rel_bias, keys, tq, tk):
    h = rel_bias.shape[1]
    w = tq + tk
    tiles = []
    for kind, v in keys:
        if kind == "far":
            c = rel_bias[_t5_bucket(jnp.full((1,), v, I32))]
            tiles.append(jnp.broadcast_to(c.reshape(h, 1, 1), (h, tq, tk)))
        else:
            rel = jnp.asarray(v + np.arange(w - 1) - (tq - 1), I32)
            g = jnp.transpose(rel_bias[_t5_bucket(rel)])
            z = jnp.concatenate([g, jnp.zeros((h, 1), g.dtype)], axis=1)
            skew = jnp.tile(z, (1, tq))[:, :tq * (w - 1)].reshape(h, tq, w - 1)
            tiles.append(skew[:, :, tq - 1:tq - 1 + tk])
    return jnp.stack(tiles, axis=1).astype(F32)


def _meta_bias_tiles(rel_bias, qpos0, nq, tq):
    n = 1
    while n < nq and qpos0 + (n - 1) * tq - (N_META - 1) < REL_MAX_DIST:
        n += 1
    qpos = qpos0 + np.arange(n * tq).reshape(n, tq)
    rel = np.arange(N_META)[None, None, :] - qpos[:, :, None]
    b = jnp.transpose(rel_bias[_t5_bucket(jnp.asarray(rel.astype(np.int32)))], (3, 0, 1, 2)).astype(F32)
    pad = jnp.full(b.shape[:3] + (LANES - N_META,), NEG, F32)
    return jnp.concatenate([b, pad], axis=-1), n


def _softmax_step(s, m_ref, l_ref, a_ref, v, first):
    m_cur = jnp.max(s, axis=-1, keepdims=True)
    if first:
        m_new = m_cur
        p = jnp.exp(s - m_new)
        l_ref[...] = jnp.sum(p, axis=-1, keepdims=True)
        a_ref[...] = jnp.dot(p.astype(v.dtype), v, preferred_element_type=F32)
    else:
        m_prev = m_ref[...]
        m_new = jnp.maximum(m_prev, m_cur)
        alpha = jnp.exp(m_prev - m_new)
        p = jnp.exp(s - m_new)
        l_ref[...] = alpha * l_ref[...] + jnp.sum(p, axis=-1, keepdims=True)
        a_ref[...] = alpha * a_ref[...] + jnp.dot(p.astype(v.dtype), v, preferred_element_type=F32)
    m_ref[...] = m_new


def _qk(q, k):
    return lax.dot_general(q, k, (((1,), (1,)), ((), ())), preferred_element_type=F32)


def _pad_meta(x):
    return jnp.concatenate([x, jnp.zeros((LANES - N_META, x.shape[1]), x.dtype)], axis=0)


def _diff_attn_kernel(tid_ref, q_ref, k_ref, v_ref, km_ref, vm_ref, bias_ref, biasm_ref,
                      lq1_ref, lk1_ref, lq2_ref, lk2_ref, sub_ref, prev_ref, o_ref,
                      m1, l1, a1, m2, l2, a2):
    del tid_ref, prev_ref
    j = pl.program_id(3)
    nk = pl.num_programs(3)
    q = q_ref[...]
    q1, q2 = q[:, :HD], q[:, HD:]

    @pl.when(j == 0)
    def _():
        km = _pad_meta(km_ref[...])
        vm = _pad_meta(vm_ref[...])
        bm = biasm_ref[0, 0]
        _softmax_step(_qk(q1, km[:, :HD]) + bm, m1, l1, a1, vm, True)
        _softmax_step(_qk(q2, km[:, HD:]) + bm, m2, l2, a2, vm, True)

    k = k_ref[...]
    v = v_ref[...]
    b = bias_ref[0, 0]
    _softmax_step(_qk(q1, k[:, :HD]) + b, m1, l1, a1, v, False)
    _softmax_step(_qk(q2, k[:, HD:]) + b, m2, l2, a2, v, False)

    @pl.when(j == nk - 1)
    def _():
        lam = (jnp.exp(jnp.sum(lq1_ref[...] * lk1_ref[...], axis=-1, keepdims=True))
               - jnp.exp(jnp.sum(lq2_ref[...] * lk2_ref[...], axis=-1, keepdims=True))
               + LAMBDA_INIT)
        o = a1[...] / l1[...] - lam * (a2[...] / l2[...])
        ms = jnp.mean(o * o, axis=-1, keepdims=True)
        o_ref[...] = (o * lax.rsqrt(ms + EPS) * sub_ref[...] * (1.0 - LAMBDA_INIT)).astype(o_ref.dtype)


def _diff_attn(proj, oa, rel_bias, lam_vecs, subln, tile_cache, *, w_a, n_seq, seq_len, kv_row0,
               q_row0, q_stride, tq, nq, qpos0, meta_row0, tk):
    h_a = w_a // (2 * HD)
    nk = seq_len // tk
    assert seq_len % tk == 0 and kv_row0 % tk == 0 and q_row0 % tq == 0 and q_stride % tq == 0
    assert meta_row0 % N_META == 0
    keys, ids = _bias_tile_plan(qpos0, nq, tq, nk, tk)
    if (keys, tq, tk) not in tile_cache:
        tile_cache[(keys, tq, tk)] = _bias_tiles(rel_bias, keys, tq, tk)
    tiles = tile_cache[(keys, tq, tk)]
    biasm, n_bm = _meta_bias_tiles(rel_bias, qpos0, nq, tq)
    qb0, qbs = q_row0 // tq, q_stride // tq
    kb0, kbs = kv_row0 // tk, seq_len // tk
    mb0 = meta_row0 // N_META
    kcol, vcol = w_a // (2 * HD), 2 * w_a // (2 * HD)
    dv = 2 * HD

    def qmap(b, h, i, j, t):
        return (qb0 + b * qbs + i, h)

    vec = pl.BlockSpec((1, HD), lambda b, h, i, j, t: (0, 0))
    grid_spec = pltpu.PrefetchScalarGridSpec(
        num_scalar_prefetch=1,
        grid=(n_seq, h_a, nq, nk),
        in_specs=[
            pl.BlockSpec((tq, dv), qmap),
            pl.BlockSpec((tk, dv), lambda b, h, i, j, t: (kb0 + b * kbs + j, kcol + h)),
            pl.BlockSpec((tk, dv), lambda b, h, i, j, t: (kb0 + b * kbs + j, vcol + h)),
            pl.BlockSpec((N_META, dv), lambda b, h, i, j, t: (mb0 + b, kcol + h)),
            pl.BlockSpec((N_META, dv), lambda b, h, i, j, t: (mb0 + b, vcol + h)),
            pl.BlockSpec((1, 1, tq, tk), lambda b, h, i, j, t: (h, t[i * nk + j], 0, 0)),
            pl.BlockSpec((1, 1, tq, LANES), lambda b, h, i, j, t: (h, jnp.minimum(i, n_bm - 1), 0, 0)),
            vec, vec, vec, vec,
            pl.BlockSpec((1, dv), lambda b, h, i, j, t: (0, 0)),
            pl.BlockSpec(memory_space=pl.ANY),
        ],
        out_specs=pl.BlockSpec((tq, dv), qmap),
        scratch_shapes=[pltpu.VMEM((tq, 1), F32), pltpu.VMEM((tq, 1), F32), pltpu.VMEM((tq, dv), F32),
                        pltpu.VMEM((tq, 1), F32), pltpu.VMEM((tq, 1), F32), pltpu.VMEM((tq, dv), F32)],
    )
    return pl.pallas_call(
        _diff_attn_kernel,
        grid_spec=grid_spec,
        out_shape=jax.ShapeDtypeStruct(oa.shape, oa.dtype),
        input_output_aliases={13: 0},
        compiler_params=_params(("parallel", "parallel", "parallel", "arbitrary")),
        name=f"diff_attn_q{tq}",
    )(jnp.asarray(ids), proj, proj, proj, proj, proj, tiles, biasm, *lam_vecs, subln, oa)


def _na_plan():
    rows_c = 3 * NA_ROWS
    a = np.arange(NA_ROWS)
    kl = np.arange(NA_WIN_ROWS)
    ridx = np.zeros((3, NA_ROWS, NA_WIN_ROWS), np.int32)
    rvalid = np.zeros((3, NA_ROWS, NA_WIN_ROWS), bool)
    for v in range(3):
        r = v * NA_ROWS + a
        rs = np.clip(r - NA_MAX_ROWS // 2, 0, rows_c - NA_MAX_ROWS)
        kr = v * NA_ROWS - NA_MAX_ROWS // 2 + kl
        off = kr[None, :] - r[:, None]
        ok = (kr[None, :] >= rs[:, None]) & (kr[None, :] < rs[:, None] + NA_MAX_ROWS)
        ok &= (kr[None, :] >= 0) & (kr[None, :] < rows_c)
        ridx[v] = np.clip(off + NA_MAX_ROWS - 1, 0, 2 * NA_MAX_ROWS - 2)
        rvalid[v] = ok
    c = np.arange(GRID_W)
    cs = np.clip(c - NA_COLS // 2, 0, GRID_W - NA_COLS)
    coff = c[None, :] - c[:, None]
    cvalid = (c[None, :] >= cs[:, None]) & (c[None, :] < cs[:, None] + NA_COLS)
    cidx = np.clip(coff + NA_COLS - 1, 0, 2 * NA_COLS - 2).astype(np.int32)
    return ridx, rvalid, cidx, cvalid


def _na_bias_tiles(rpb):
    ridx, rvalid, cidx, cvalid = _na_plan()
    h = rpb.shape[0]
    t1 = rpb[:, :, cidx]
    t2 = t1[:, ridx]
    valid = rvalid[:, :, :, None, None] & cvalid[None, None, None]
    t2 = jnp.where(jnp.asarray(valid)[None], t2, NEG)
    t2 = jnp.transpose(t2, (0, 1, 2, 4, 3, 5))
    return t2.reshape(h, 3, NA_BLOCK, NA_WIN).astype(F32)


def _na_kernel(q_ref, k0_ref, k1_ref, k2_ref, k3_ref, v0_ref, v1_ref, v2_ref, v3_ref, km_ref, vm_ref,
               bias_ref, prev_ref, o_ref):
    del prev_ref
    q = q_ref[...]
    k = jnp.concatenate([k0_ref[...], k1_ref[...], k2_ref[...], k3_ref[...]], axis=0)
    v = jnp.concatenate([v0_ref[...], v1_ref[...], v2_ref[...], v3_ref[...]], axis=0)
    s = _qk(q, k) + bias_ref[0, 0]
    km = _pad_meta(km_ref[...])
    vm = _pad_meta(vm_ref[...])
    lane = lax.broadcasted_iota(I32, (q.shape[0], LANES), 1)
    sm = jnp.where(lane < N_META, _qk(q, km), NEG)
    m = jnp.maximum(jnp.max(s, axis=-1, keepdims=True), jnp.max(sm, axis=-1, keepdims=True))
    p = jnp.exp(s - m)
    pm = jnp.exp(sm - m)
    l = jnp.sum(p, axis=-1, keepdims=True) + jnp.sum(pm, axis=-1, keepdims=True)
    o = (jnp.dot(p.astype(v.dtype), v, preferred_element_type=F32)
         + jnp.dot(pm.astype(vm.dtype), vm, preferred_element_type=F32))
    o_ref[...] = (o / l).astype(o_ref.dtype)


def _na_attn(proj, ob, bias, *, w_a, w_b, n_seq, seq_len, row0, meta_row0):
    h_b = w_b // HD
    nblk = seq_len // NA_BLOCK
    assert seq_len % NA_BLOCK == 0 and nblk >= 2 and row0 % NA_BLOCK == 0
    rb0 = row0 // NA_BLOCK
    mb0 = meta_row0 // N_META
    qcol, kcol, vcol = 3 * w_a // HD, (3 * w_a + w_b) // HD, (3 * w_a + 2 * w_b) // HD

    per_q = NA_BLOCK // NA_KBLOCK
    n_kb = nblk * per_q

    def kblk(col, t):
        def index(h, b, i):
            return (rb0 * per_q + b * n_kb + jnp.clip(i * per_q - 1 + t, 0, n_kb - 1), col + h)
        return pl.BlockSpec((NA_KBLOCK, HD), index)

    def variant(i):
        return jnp.where(i == 0, 0, jnp.where(i == nblk - 1, 2, 1))

    n_win = NA_WIN // NA_KBLOCK
    return pl.pallas_call(
        _na_kernel,
        grid=(h_b, n_seq, nblk),
        in_specs=[pl.BlockSpec((NA_BLOCK, HD), lambda h, b, i: (rb0 + b * nblk + i, qcol + h))]
                 + [kblk(kcol, t) for t in range(n_win)] + [kblk(vcol, t) for t in range(n_win)]
                 + [pl.BlockSpec((N_META, HD), lambda h, b, i: (mb0 + b, kcol + h)),
                    pl.BlockSpec((N_META, HD), lambda h, b, i: (mb0 + b, vcol + h)),
                    pl.BlockSpec((1, 1, NA_BLOCK, NA_WIN), lambda h, b, i: (h, variant(i), 0, 0)),
                    pl.BlockSpec(memory_space=pl.ANY)],
        out_specs=pl.BlockSpec((NA_BLOCK, HD), lambda h, b, i: (rb0 + b * nblk + i, h)),
        out_shape=jax.ShapeDtypeStruct(ob.shape, ob.dtype),
        input_output_aliases={2 * n_win + 4: 0},
        compiler_params=_params(("parallel", "parallel", "parallel")),
        name="na_attn",
    )(*([proj] * (2 * n_win + 3)), bias, ob)


def _na_meta_kernel(q_ref, km_ref, vm_ref, prev_ref, o_ref):
    del prev_ref
    q = jnp.concatenate([q_ref[...], jnp.zeros((LANES - N_META, HD), q_ref.dtype)], axis=0)
    km = _pad_meta(km_ref[...])
    vm = _pad_meta(vm_ref[...])
    lane = lax.broadcasted_iota(I32, (LANES, LANES), 1)
    s = jnp.where(lane < N_META, _qk(q, km), NEG)
    m = jnp.max(s, axis=-1, keepdims=True)
    p = jnp.exp(s - m)
    o = jnp.dot(p.astype(vm.dtype), vm, preferred_element_type=F32) / jnp.sum(p, axis=-1, keepdims=True)
    o_ref[...] = o[:N_META].astype(o_ref.dtype)


def _na_meta_attn(proj, ob, *, w_a, w_b, n_seq, meta_row0):
    h_b = w_b // HD
    mb0 = meta_row0 // N_META
    qcol, kcol, vcol = 3 * w_a // HD, (3 * w_a + w_b) // HD, (3 * w_a + 2 * w_b) // HD
    return pl.pallas_call(
        _na_meta_kernel,
        grid=(n_seq, h_b),
        in_specs=[pl.BlockSpec((N_META, HD), lambda s, h: (mb0 + s, qcol + h)),
                  pl.BlockSpec((N_META, HD), lambda s, h: (mb0 + s, kcol + h)),
                  pl.BlockSpec((N_META, HD), lambda s, h: (mb0 + s, vcol + h)),
                  pl.BlockSpec(memory_space=pl.ANY)],
        out_specs=pl.BlockSpec((N_META, HD), lambda s, h: (mb0 + s, h)),
        out_shape=jax.ShapeDtypeStruct(ob.shape, ob.dtype),
        input_output_aliases={3: 0},
        compiler_params=_params(("parallel", "parallel")),
        name="na_meta_attn",
    )(proj, proj, proj, ob)


def _merge_kernel(oa_ref, ob_ref, wa_ref, wb_ref, ga_ref, gb_ref, o_ref):
    a = jnp.dot(oa_ref[...], wa_ref[...], preferred_element_type=F32)
    b = jnp.dot(ob_ref[...], wb_ref[...], preferred_element_type=F32)
    o_ref[...] = (ga_ref[...].astype(F32) * a + gb_ref[...].astype(F32) * b).astype(o_ref.dtype)


def _merge(oa, ob, wa, wb, proj, gate_col0):
    m, w_a = oa.shape
    w_b = ob.shape[1]
    d = wa.shape[1]
    tm = min(ROW_TILE, m)
    tn = min(COL_TILE, d)
    ga0 = gate_col0 // tn
    gb0 = (gate_col0 + d) // tn
    return pl.pallas_call(
        _merge_kernel,
        grid=(d // tn, m // tm),
        in_specs=[pl.BlockSpec((tm, w_a), lambda j, i: (i, 0)),
                  pl.BlockSpec((tm, w_b), lambda j, i: (i, 0)),
                  pl.BlockSpec((w_a, tn), lambda j, i: (0, j)),
                  pl.BlockSpec((w_b, tn), lambda j, i: (0, j)),
                  pl.BlockSpec((tm, tn), lambda j, i: (i, ga0 + j)),
                  pl.BlockSpec((tm, tn), lambda j, i: (i, gb0 + j))],
        out_specs=pl.BlockSpec((tm, tn), lambda j, i: (i, j)),
        out_shape=jax.ShapeDtypeStruct((m, d), BF16),
        compiler_params=_params(("parallel", "parallel")),
        name="merge",
    )(oa, ob, wa, wb, proj, proj)


def _outproj_kernel(x_ref, w_ref, h_ref, o_ref):
    o_ref[...] = h_ref[...] + jnp.dot(x_ref[...], w_ref[...], preferred_element_type=F32)


def _outproj(merged, w, h):
    m, d = merged.shape
    n = w.shape[1]
    tm = min(ROW_TILE, m)
    tn = min(COL_TILE, n)
    return pl.pallas_call(
        _outproj_kernel,
        grid=(n // tn, m // tm),
        in_specs=[pl.BlockSpec((tm, d), lambda j, i: (i, 0)),
                  pl.BlockSpec((d, tn), lambda j, i: (0, j)),
                  pl.BlockSpec((tm, tn), lambda j, i: (i, j))],
        out_specs=pl.BlockSpec((tm, tn), lambda j, i: (i, j)),
        out_shape=jax.ShapeDtypeStruct((m, n), F32),
        compiler_params=_params(("parallel", "parallel")),
        name="outproj",
    )(merged, w, h)


def _pack_bf16_pairs(x):
    half = x.shape[1] // 2
    lo = lax.bitcast_convert_type(x[:, :half].astype(jnp.bfloat16).astype(F32), U32)
    hi = lax.bitcast_convert_type(x[:, half:].astype(jnp.bfloat16).astype(F32), U32)
    return (lo >> 16) | (hi & jnp.uint32(0xFFFF0000))


def _unpack_bf16_pairs(u):
    lo = lax.bitcast_convert_type(u << 16, F32).astype(BF16)
    hi = lax.bitcast_convert_type(u & jnp.uint32(0xFFFF0000), F32).astype(BF16)
    return jnp.concatenate([lo, hi], axis=1)


def _router_kernel(h_ref, g_ref, wr_ref, br_ref, hf_ref, ti_ref, tw_ref):
    x = h_ref[...]
    ms = jnp.mean(x * x, axis=-1, keepdims=True)
    hf = x * lax.rsqrt(ms + EPS) * g_ref[...]
    hf_ref[...] = _pack_bf16_pairs(hf)
    logits = jnp.dot(hf, wr_ref[...], preferred_element_type=F32,
                     precision=lax.Precision.HIGHEST) + br_ref[...]
    lane = lax.broadcasted_iota(I32, logits.shape, 1).astype(F32)
    vals, idxs = [], []
    for _ in range(TOP_K):
        mx = jnp.max(logits, axis=-1, keepdims=True)
        ix = jnp.min(jnp.where(logits == mx, lane, float(LANES)), axis=-1, keepdims=True)
        vals.append(mx)
        idxs.append(ix)
        logits = jnp.where(lane == ix, -jnp.inf, logits)
    es = [jnp.exp(v - vals[0]) for v in vals]
    tot = es[0]
    for e in es[1:]:
        tot = tot + e
    ti = jnp.zeros(lane.shape, F32)
    tw = jnp.zeros(lane.shape, F32)
    for k in range(TOP_K):
        ti = jnp.where(lane == k, idxs[k], ti)
        tw = jnp.where(lane == k, es[k] / tot, tw)
    ti_ref[...] = ti.astype(I32)
    tw_ref[...] = tw


def _router(h1, g, w_router, b_router):
    m, d = h1.shape
    e = w_router.shape[1]
    assert e <= LANES
    tm = min(NORM_TILE, m)
    wr = jnp.zeros((d, LANES), F32).at[:, :e].set(w_router.astype(F32))
    br = jnp.full((1, LANES), NEG, F32).at[0, :e].set(b_router.astype(F32))
    return pl.pallas_call(
        _router_kernel,
        grid=(m // tm,),
        in_specs=[pl.BlockSpec((tm, d), lambda i: (i, 0)),
                  pl.BlockSpec((1, d), lambda i: (0, 0)),
                  pl.BlockSpec((d, LANES), lambda i: (0, 0)),
                  pl.BlockSpec((1, LANES), lambda i: (0, 0))],
        out_specs=[pl.BlockSpec((tm, d // 2), lambda i: (i, 0)),
                   pl.BlockSpec((tm, LANES), lambda i: (i, 0)),
                   pl.BlockSpec((tm, LANES), lambda i: (i, 0))],
        out_shape=[jax.ShapeDtypeStruct((m, d // 2), U32),
                   jax.ShapeDtypeStruct((m, LANES), I32),
                   jax.ShapeDtypeStruct((m, LANES), F32)],
        compiler_params=_params(("parallel",)),
        name="router",
    )(h1, g.reshape(1, d).astype(F32), wr, br)


def _row_copy(src_ref, src_row, dst_ref, dst_row, sem):
    return pltpu.make_async_copy(src_ref.at[pl.ds(src_row, 1)], dst_ref.at[pl.ds(dst_row, 1)], sem)


def _gather_kernel(nu_ref, idx_ref, src_ref, o_ref, sem):
    rows = o_ref.shape[0]

    def issue(r, c):
        _row_copy(src_ref, idx_ref[0, r], o_ref, r, sem).start()
        return c

    def drain(r, c):
        _row_copy(src_ref, 0, o_ref, r, sem).wait()
        return c

    @pl.when(pl.program_id(0) < nu_ref[0])
    def _():
        lax.fori_loop(0, rows, issue, 0, unroll=DMA_UNROLL)
        lax.fori_loop(0, rows, drain, 0, unroll=DMA_UNROLL)

    @pl.when(pl.program_id(0) >= nu_ref[0])
    def _():
        o_ref[...] = jnp.zeros(o_ref.shape, o_ref.dtype)


def _gather_rows(src, slot_rows, n_used, tile):
    n_slots = slot_rows.shape[0]
    width = src.shape[1]
    nb = n_slots // tile
    return pl.pallas_call(
        _gather_kernel,
        grid=(nb,),
        in_specs=[pl.BlockSpec(memory_space=pltpu.SMEM),
                  pl.BlockSpec((None, 1, tile), lambda b: (b, 0, 0), memory_space=pltpu.SMEM),
                  pl.BlockSpec(memory_space=pl.ANY)],
        out_specs=pl.BlockSpec((tile, width), lambda b: (b, 0)),
        out_shape=jax.ShapeDtypeStruct((n_slots, width), src.dtype),
        scratch_shapes=[pltpu.SemaphoreType.DMA(())],
        compiler_params=_params(("arbitrary",)),
        name="moe_gather",
    )(n_used, slot_rows.reshape(nb, 1, tile), src)


def _moe_kernel(be_ref, nu_ref, x_ref, wg_ref, wu_ref, wd_ref, bg_ref, bu_ref, bd_ref, o_ref, xb):
    del be_ref
    b = pl.program_id(0)
    f = pl.program_id(1)

    @pl.when(b < nu_ref[0])
    def _():
        @pl.when(f == 0)
        def _():
            xb[...] = _unpack_bf16_pairs(x_ref[...])
            o_ref[...] = jnp.broadcast_to(bd_ref[0], o_ref.shape)

        x = xb[...]
        g = jnp.dot(x, wg_ref[0], preferred_element_type=F32) + bg_ref[0]
        u = jnp.dot(x, wu_ref[0], preferred_element_type=F32) + bu_ref[0]
        g = jnp.minimum(g, SWIGLU_LIMIT)
        u = jnp.clip(u, -SWIGLU_LIMIT, SWIGLU_LIMIT)
        act = (u + 1.0) * (g * (1.0 / (1.0 + jnp.exp(-SWIGLU_ALPHA * g))))
        o_ref[...] += jnp.dot(act.astype(BF16), wd_ref[0], preferred_element_type=F32)

    @pl.when(jnp.logical_and(b >= nu_ref[0], f == 0))
    def _():
        o_ref[...] = jnp.zeros(o_ref.shape, o_ref.dtype)


def _moe_experts(xs, block_e, n_used, wg, wu, wd, bg, bu, bd, tile):
    n_slots, half = xs.shape
    d = 2 * half
    e, _, ff = wg.shape
    tf = min(MOE_FF_TILE, ff)
    nb, nf = n_slots // tile, ff // tf

    def bb(b, nu):
        return jnp.minimum(b, nu[0] - 1)

    def fidx(b, f, nu):
        return jnp.where(b < nu[0], f, nf - 1)

    grid_spec = pltpu.PrefetchScalarGridSpec(
        num_scalar_prefetch=2,
        grid=(nb, nf),
        in_specs=[
            pl.BlockSpec((tile, half), lambda b, f, be, nu: (bb(b, nu), 0)),
            pl.BlockSpec((1, d, tf), lambda b, f, be, nu: (be[bb(b, nu)], 0, fidx(b, f, nu))),
            pl.BlockSpec((1, d, tf), lambda b, f, be, nu: (be[bb(b, nu)], 0, fidx(b, f, nu))),
            pl.BlockSpec((1, tf, d), lambda b, f, be, nu: (be[bb(b, nu)], fidx(b, f, nu), 0)),
            pl.BlockSpec((1, 1, tf), lambda b, f, be, nu: (be[bb(b, nu)], 0, fidx(b, f, nu))),
            pl.BlockSpec((1, 1, tf), lambda b, f, be, nu: (be[bb(b, nu)], 0, fidx(b, f, nu))),
            pl.BlockSpec((1, 1, d), lambda b, f, be, nu: (be[bb(b, nu)], 0, 0)),
        ],
        out_specs=pl.BlockSpec((tile, d), lambda b, f, be, nu: (b, 0)),
        scratch_shapes=[pltpu.VMEM((tile, d), BF16)],
    )
    return pl.pallas_call(
        _moe_kernel,
        grid_spec=grid_spec,
        out_shape=jax.ShapeDtypeStruct((n_slots, d), F32),
        compiler_params=_params(("arbitrary", "arbitrary")),
        name="moe_experts",
    )(block_e, n_used, xs, wg, wu, wd, bg.reshape(e, 1, ff), bu.reshape(e, 1, ff), bd.reshape(e, 1, d))


def _combine_kernel(pos_ref, w_ref, h_ref, ys_ref, o_ref, buf, sem):
    rows = o_ref.shape[0]

    def issue(r, c):
        for k in range(TOP_K):
            _row_copy(ys_ref, pos_ref[0, TOP_K * r + k], buf.at[k], r, sem).start()
        return c

    def drain(r, c):
        for k in range(TOP_K):
            _row_copy(ys_ref, 0, buf.at[k], r, sem).wait()
        return c

    lax.fori_loop(0, rows, issue, 0, unroll=DMA_UNROLL // 2)
    lax.fori_loop(0, rows, drain, 0, unroll=DMA_UNROLL // 2)
    w = w_ref[...]
    acc = h_ref[...]
    for k in range(TOP_K):
        acc = acc + w[:, k:k + 1] * buf[k]
    o_ref[...] = acc


def _combine(h1, tw, pos, ys, row0, n_rows):
    d = h1.shape[1]
    tc = min(COMBINE_TILE, n_rows)
    assert n_rows % tc == 0 and row0 % tc == 0
    nb, b0 = n_rows // tc, row0 // tc
    pos_blocks = pos[row0:row0 + n_rows].reshape(nb, 1, tc * TOP_K)
    return pl.pallas_call(
        _combine_kernel,
        grid=(nb,),
        in_specs=[pl.BlockSpec((None, 1, tc * TOP_K), lambda i: (i, 0, 0), memory_space=pltpu.SMEM),
                  pl.BlockSpec((tc, LANES), lambda i: (b0 + i, 0)),
                  pl.BlockSpec((tc, d), lambda i: (b0 + i, 0)),
                  pl.BlockSpec(memory_space=pl.ANY)],
        out_specs=pl.BlockSpec((tc, d), lambda i: (i, 0)),
        out_shape=jax.ShapeDtypeStruct((n_rows, d), F32),
        scratch_shapes=[pltpu.VMEM((TOP_K, tc, d), F32), pltpu.SemaphoreType.DMA(())],
        compiler_params=_params(("arbitrary",)),
        name="moe_combine",
    )(pos_blocks, tw, h1, ys)


def _dispatch_plan(top_i, n_experts, tile):
    n_tok = top_i.shape[0]
    nk = n_tok * TOP_K
    flat_e = top_i.reshape(-1)
    onehot = (flat_e[:, None] == jnp.arange(n_experts, dtype=I32)[None, :]).astype(I32)
    rank = jnp.take_along_axis(jnp.cumsum(onehot, axis=0), flat_e[:, None], axis=1)[:, 0] - 1
    counts = jnp.sum(onehot, axis=0)
    nblk = (counts + tile - 1) // tile
    blk_end = jnp.cumsum(nblk)
    blk_start = blk_end - nblk
    pos = blk_start[flat_e] * tile + rank
    nb_max = -(-nk // tile) + n_experts
    block_e = jnp.clip(jnp.searchsorted(blk_end, jnp.arange(nb_max, dtype=I32), side="right"),
                       0, n_experts - 1).astype(I32)
    n_used = blk_end[-1:].astype(I32)
    return pos.astype(I32), block_e, n_used, nb_max


def _round_up(x, m):
    return -(-x // m) * m


def kernel(x_prompt, x_sample, meta_tokens, norm_mix, w_in, qk_norm_a_q, qk_norm_a_k, lambda_q1, lambda_k1, lambda_q2, lambda_k2, subln_a, rel_bias, qk_norm_b_q, qk_norm_b_k, na_rpb, w_br_a, w_br_b, w_out, norm_ffn, w_router, b_router, w_gate, b_gate, w_up, b_up, w_down, b_down):
    d = x_prompt.shape[-1]
    w_a = w_br_a.shape[1]
    w_b = w_br_b.shape[1]
    n_experts = w_router.shape[-1]
    groups = [x_prompt, x_sample]

    row0s, seqs = [], []
    m_real = 0
    for x in groups:
        row0s.append(m_real)
        m_real += x.shape[0] * x.shape[1]
    n_seq = sum(x.shape[0] for x in groups)
    meta_row0 = m_real
    m_tok = m_real + n_seq * N_META
    m_pad = _round_up(m_tok + 1, ROW_TILE)
    h_all = jnp.concatenate(
        [x.reshape(-1, d) for x in groups]
        + [jnp.tile(meta_tokens.astype(F32), (n_seq, 1)), jnp.zeros((m_pad - m_tok, d), F32)], axis=0)

    hn = _rmsnorm_rows(h_all, norm_mix[0], BF16)
    scale_a, scale_b = HD ** -0.5, HD ** -0.5
    ones = lambda n: jnp.ones((n,), F32)
    gain = jnp.concatenate([
        jnp.tile(qk_norm_a_q[0].astype(F32) * scale_a, w_a // HD), jnp.tile(qk_norm_a_k[0].astype(F32), w_a // HD),
        ones(w_a),
        jnp.tile(qk_norm_b_q[0].astype(F32) * scale_b, w_b // HD), jnp.tile(qk_norm_b_k[0].astype(F32), w_b // HD),
        ones(w_b + 2 * d)]).reshape(1, -1)
    proj = _inproj(hn, w_in[0].astype(BF16), gain, w_a, w_b)

    lam_vecs = [v[0].reshape(1, HD).astype(F32) for v in (lambda_q1, lambda_k1, lambda_q2, lambda_k2)]
    subln = subln_a[0].reshape(1, 2 * HD).astype(F32)
    na_bias = _na_bias_tiles(na_rpb[0].astype(F32))
    oa = jnp.zeros((m_pad, w_a), BF16)
    ob = jnp.zeros((m_pad, w_b), BF16)
    tile_cache = {}
    seq0 = 0
    for x, row0 in zip(groups, row0s):
        nb, s = x.shape[0], x.shape[1]
        mrow = meta_row0 + seq0 * N_META
        tq = min(ATTN_TQ, s)
        common = dict(w_a=w_a, n_seq=nb, seq_len=s, kv_row0=row0, meta_row0=mrow, tk=min(ATTN_TK, s))
        oa = _diff_attn(proj, oa, rel_bias.astype(F32), lam_vecs, subln, tile_cache, q_row0=row0,
                        q_stride=s, tq=tq, nq=s // tq, qpos0=N_META, **common)
        oa = _diff_attn(proj, oa, rel_bias.astype(F32), lam_vecs, subln, tile_cache, q_row0=mrow,
                        q_stride=N_META, tq=N_META, nq=1, qpos0=0, **common)
        ob = _na_attn(proj, ob, na_bias, w_a=w_a, w_b=w_b, n_seq=nb, seq_len=s, row0=row0, meta_row0=mrow)
        seq0 += nb
    ob = _na_meta_attn(proj, ob, w_a=w_a, w_b=w_b, n_seq=n_seq, meta_row0=meta_row0)

    merged = _merge(oa, ob, w_br_a[0].astype(BF16), w_br_b[0].astype(BF16), proj, 3 * w_a + 3 * w_b)
    h1 = _outproj(merged, w_out[0].astype(BF16), h_all)

    hf, top_i, top_w = _router(h1, norm_ffn[0], w_router[0], b_router[0])
    pos, block_e, n_used, nb_max = _dispatch_plan(top_i[:m_tok, :TOP_K], n_experts, MOE_TILE)
    slot_rows = jnp.full((nb_max * MOE_TILE,), m_tok, I32).at[pos].set(
        jnp.repeat(jnp.arange(m_tok, dtype=I32), TOP_K))
    xs = _gather_rows(hf, slot_rows, n_used, MOE_TILE)
    ys = _moe_experts(xs, block_e, n_used, w_gate[0].astype(BF16), w_up[0].astype(BF16),
                      w_down[0].astype(BF16), b_gate[0].astype(F32), b_up[0].astype(F32),
                      b_down[0].astype(F32), MOE_TILE)
    pos2 = pos.reshape(m_tok, TOP_K)
    outs = []
    for x, row0 in zip(groups, row0s):
        n_rows = x.shape[0] * x.shape[1]
        outs.append(_combine(h1, top_w, pos2, ys, row0, n_rows).reshape(x.shape))
    return tuple(outs)
```

```python
import functools
import math

import numpy as np
import jax
import jax.numpy as jnp
from jax import lax
from jax.experimental import pallas as pl
from jax.experimental.pallas import tpu as pltpu

F32 = jnp.float32
BF16 = jnp.bfloat16
U32 = jnp.uint32
I32 = jnp.int32

N_META = 16
GRID_W = 64
HD = 128
REL_BUCKETS = 32
REL_MAX_DIST = 128
NA_MAX_ROWS = 8
NA_COLS = 16
TOP_K = 4
SWIGLU_LIMIT = 7.0
SWIGLU_ALPHA = 1.702
EPS = 1e-6
LAMBDA_INIT = 0.8 - 0.6 * math.exp(-0.3 * 0)
NEG = -1e30
LOG2E = math.log2(math.e)

LANES = 128
VMEM_LIMIT_MB = 56
ROW_TILE = 1024
COL_TILE = 1024
NORM_TILE = 256
NA_ROWS = 8
NA_BLOCK = NA_ROWS * GRID_W
NA_WIN_ROWS = NA_ROWS + NA_MAX_ROWS
NA_WIN = NA_WIN_ROWS * GRID_W
NA_KBLOCK = NA_WIN // 4
MOE_TILE = 1024
MOE_FF_TILE = 256
MOE_DOWN_TILE = 512
ATTN_TQ = 1024
ATTN_TK = 1024
COMBINE_TILE = 128
DMA_UNROLL = 8


def _params(semantics, vmem_mb=VMEM_LIMIT_MB):
    return pltpu.CompilerParams(dimension_semantics=semantics,
                                vmem_limit_bytes=vmem_mb * 2**20)


def _rmsnorm_kernel(x_ref, g_ref, o_ref):
    x = x_ref[...].astype(F32)
    ms = jnp.mean(x * x, axis=-1, keepdims=True)
    o_ref[...] = (x * lax.rsqrt(ms + EPS) * g_ref[...]).astype(o_ref.dtype)


def _rmsnorm_rows(x, g, out_dtype):
    m, d = x.shape
    tm = min(NORM_TILE, m)
    return pl.pallas_call(
        _rmsnorm_kernel,
        grid=(m // tm,),
        in_specs=[pl.BlockSpec((tm, d), lambda i: (i, 0)),
                  pl.BlockSpec((1, d), lambda i: (0, 0))],
        out_specs=pl.BlockSpec((tm, d), lambda i: (i, 0)),
        out_shape=jax.ShapeDtypeStruct((m, d), out_dtype),
        compiler_params=_params(("parallel",)),
        name="rmsnorm_rows",
    )(x, g.reshape(1, d).astype(F32))


def _inproj_kernel(x_ref, w_ref, gain_ref, o_ref, *, norm_ranges, gate_start):
    j = pl.program_id(0)
    acc = jnp.dot(x_ref[...], w_ref[...], preferred_element_type=F32)
    tn = acc.shape[1]

    is_norm = False
    for lo, hi in norm_ranges:
        is_norm = jnp.logical_or(is_norm, jnp.logical_and(j >= lo, j < hi))
    is_gate = j >= gate_start

    @pl.when(is_norm)
    def _():
        for c in range(tn // HD):
            blk = acc[:, c * HD:(c + 1) * HD]
            ms = jnp.mean(blk * blk, axis=-1, keepdims=True)
            o_ref[:, c * HD:(c + 1) * HD] = (
                blk * lax.rsqrt(ms + EPS) * gain_ref[:, c * HD:(c + 1) * HD]).astype(o_ref.dtype)

    @pl.when(is_gate)
    def _():
        o_ref[...] = (1.0 / (1.0 + jnp.exp(-acc))).astype(o_ref.dtype)

    @pl.when(jnp.logical_not(jnp.logical_or(is_norm, is_gate)))
    def _():
        o_ref[...] = acc.astype(o_ref.dtype)


def _inproj(hn, w, gain, w_a, w_b):
    m, d = hn.shape
    n = w.shape[1]
    tm = min(ROW_TILE, m)
    tn = min(COL_TILE, w_a, w_b)
    assert w_a % tn == 0 and w_b % tn == 0 and n % tn == 0 and m % tm == 0
    norm_ranges = ((0, 2 * w_a // tn), (3 * w_a // tn, (3 * w_a + 2 * w_b) // tn))
    gate_start = (3 * w_a + 3 * w_b) // tn
    kern = functools.partial(_inproj_kernel, norm_ranges=norm_ranges, gate_start=gate_start)
    return pl.pallas_call(
        kern,
        grid=(n // tn, m // tm),
        in_specs=[pl.BlockSpec((tm, d), lambda j, i: (i, 0)),
                  pl.BlockSpec((d, tn), lambda j, i: (0, j)),
                  pl.BlockSpec((1, tn), lambda j, i: (0, j))],
        out_specs=pl.BlockSpec((tm, tn), lambda j, i: (i, j)),
        out_shape=jax.ShapeDtypeStruct((m, n), BF16),
        compiler_params=_params(("parallel", "parallel")),
        name="inproj",
    )(hn, w, gain)


def _t5_bucket(rel):
    half = REL_BUCKETS // 2
    max_exact = half // 2
    sign = (rel > 0).astype(I32) * half
    n = jnp.abs(rel)
    nf = jnp.maximum(n, 1).astype(F32)
    large = max_exact + (jnp.log(nf / max_exact) / math.log(REL_MAX_DIST / max_exact)
                         * (half - max_exact)).astype(I32)
    large = jnp.minimum(large, half - 1)
    return sign + jnp.where(n < max_exact, n, large)


def _bias_tile_plan(qpos0, nq, tq, nk, tk):
    keys, ids = [], np.zeros((nq, nk), np.int32)
    for i in range(nq):
        for j in range(nk):
            d = (N_META + j * tk) - (qpos0 + i * tq)
            if d - (tq - 1) >= REL_MAX_DIST:
                key = ("far", REL_MAX_DIST)
            elif d + (tk - 1) <= -REL_MAX_DIST:
                key = ("far", -REL_MAX_DIST)
            else:
                key = ("near", d)
            if key not in keys:
                keys.append(key)
            ids[i, j] = keys.index(key)
    return tuple(keys), ids.reshape(-1)


def _bias_tiles(rel_bias, keys, tq, tk):
    h = rel_bias.shape[1]
    blocked = tq % LANES == 0 and tk % LANES == 0 and all(v % LANES == 0 for _, v in keys)
    if not blocked:
        ramp = np.arange(tk)[None, :] - np.arange(tq)[:, None]
        rel = np.stack([np.full((tq, tk), v) if kind == "far" else v + ramp for kind, v in keys])
        b = rel_bias[_t5_bucket(jnp.asarray(rel.astype(np.int32)))]
        return jnp.transpose(b, (3, 0, 1, 2)).astype(F32) * LOG2E

    def const(v):
        return rel_bias[_t5_bucket(jnp.full((1,), v, I32))].reshape(h, 1, 1, 1, 1)

    r = np.arange(LANES)
    rel3 = np.stack([LANES * dl + r[None, :] - r[:, None] for dl in (-1, 0, 1)]).astype(np.int32)
    f3 = jnp.transpose(rel_bias[_t5_bucket(jnp.asarray(rel3))], (3, 0, 1, 2))
    na, nc = tq // LANES, tk // LANES
    tiles = []
    for kind, v in keys:
        if kind == "far":
            tiles.append(jnp.broadcast_to(const(v).reshape(h, 1, 1), (h, tq, tk)))
            continue
        delta = v // LANES + np.arange(nc)[None, :] - np.arange(na)[:, None]
        t5 = jnp.where(jnp.asarray(delta < 0)[None, :, None, :, None], const(-REL_MAX_DIST), const(REL_MAX_DIST))
        for dl in (-1, 0, 1):
            t5 = jnp.where(jnp.asarray(delta == dl)[None, :, None, :, None], f3[:, dl + 1][:, None, :, None, :], t5)
        tiles.append(t5.reshape(h, tq, tk))
    return jnp.stack(tiles, axis=1).astype(F32) * LOG2E


def _meta_bias_tiles(rel_bias, qpos0, nq, tq):
    n = 1
    while n < nq and qpos0 + (n - 1) * tq - (N_META - 1) < REL_MAX_DIST:
        n += 1
    qpos = qpos0 + np.arange(n * tq).reshape(n, tq)
    rel = np.arange(N_META)[None, None, :] - qpos[:, :, None]
    b = jnp.transpose(rel_bias[_t5_bucket(jnp.asarray(rel.astype(np.int32)))], (3, 0, 1, 2)).astype(F32)
    pad = jnp.full(b.shape[:3] + (LANES - N_META,), NEG, F32)
    return jnp.concatenate([b * LOG2E, pad], axis=-1), n


def _softmax_step(s, m_ref, l_ref, a_ref, v, first):
    m_cur = jnp.max(s, axis=-1, keepdims=True)
    if first:
        m_new = m_cur
        p = jnp.exp2(s - m_new)
        l_ref[...] = jnp.sum(p, axis=-1, keepdims=True)
        a_ref[...] = jnp.dot(p.astype(v.dtype), v, preferred_element_type=F32)
    else:
        m_prev = m_ref[...]
        m_new = jnp.maximum(m_prev, m_cur)
        alpha = jnp.exp2(m_prev - m_new)
        p = jnp.exp2(s - m_new)
        l_ref[...] = alpha * l_ref[...] + jnp.sum(p, axis=-1, keepdims=True)
        a_ref[...] = alpha * a_ref[...] + jnp.dot(p.astype(v.dtype), v, preferred_element_type=F32)
    m_ref[...] = m_new


def _qk(q, k):
    return lax.dot_general(q, k, (((1,), (1,)), ((), ())), preferred_element_type=F32)


def _pad_meta(x):
    return jnp.concatenate([x, jnp.zeros((LANES - N_META, x.shape[1]), x.dtype)], axis=0)


def _diff_attn_kernel(tid_ref, q_ref, k_ref, v_ref, km_ref, vm_ref, bias_ref, biasm_ref,
                      lq1_ref, lk1_ref, lq2_ref, lk2_ref, sub_ref, prev_ref, o_ref,
                      m1, l1, a1, m2, l2, a2):
    del tid_ref, prev_ref
    j = pl.program_id(3)
    nk = pl.num_programs(3)
    q = q_ref[...]
    q1, q2 = q[:, :HD], q[:, HD:]

    @pl.when(j == 0)
    def _():
        km = _pad_meta(km_ref[...])
        vm = _pad_meta(vm_ref[...])
        bm = biasm_ref[0, 0]
        _softmax_step(_qk(q1, km[:, :HD]) + bm, m1, l1, a1, vm, True)
        _softmax_step(_qk(q2, km[:, HD:]) + bm, m2, l2, a2, vm, True)

    k = k_ref[...]
    v = v_ref[...]
    b = bias_ref[0, 0]
    _softmax_step(_qk(q1, k[:, :HD]) + b, m1, l1, a1, v, False)
    _softmax_step(_qk(q2, k[:, HD:]) + b, m2, l2, a2, v, False)

    @pl.when(j == nk - 1)
    def _():
        lam = (jnp.exp(jnp.sum(lq1_ref[...] * lk1_ref[...], axis=-1, keepdims=True))
               - jnp.exp(jnp.sum(lq2_ref[...] * lk2_ref[...], axis=-1, keepdims=True))
               + LAMBDA_INIT)
        o = a1[...] / l1[...] - lam * (a2[...] / l2[...])
        ms = jnp.mean(o * o, axis=-1, keepdims=True)
        o_ref[...] = (o * lax.rsqrt(ms + EPS) * sub_ref[...] * (1.0 - LAMBDA_INIT)).astype(o_ref.dtype)


def _diff_attn(proj, oa, rel_bias, lam_vecs, subln, tile_cache, *, w_a, n_seq, seq_len, kv_row0,
               q_row0, q_stride, tq, nq, qpos0, meta_row0, tk):
    h_a = w_a // (2 * HD)
    nk = seq_len // tk
    assert seq_len % tk == 0 and kv_row0 % tk == 0 and q_row0 % tq == 0 and q_stride % tq == 0
    assert meta_row0 % N_META == 0
    keys, ids = _bias_tile_plan(qpos0, nq, tq, nk, tk)
    if (keys, tq, tk) not in tile_cache:
        tile_cache[(keys, tq, tk)] = _bias_tiles(rel_bias, keys, tq, tk)
    tiles = tile_cache[(keys, tq, tk)]
    biasm, n_bm = _meta_bias_tiles(rel_bias, qpos0, nq, tq)
    qb0, qbs = q_row0 // tq, q_stride // tq
    kb0, kbs = kv_row0 // tk, seq_len // tk
    mb0 = meta_row0 // N_META
    kcol, vcol = w_a // (2 * HD), 2 * w_a // (2 * HD)
    dv = 2 * HD

    def qmap(b, h, i, j, t):
        return (qb0 + b * qbs + i, h)

    vec = pl.BlockSpec((1, HD), lambda b, h, i, j, t: (0, 0))
    grid_spec = pltpu.PrefetchScalarGridSpec(
        num_scalar_prefetch=1,
        grid=(n_seq, h_a, nq, nk),
        in_specs=[
            pl.BlockSpec((tq, dv), qmap),
            pl.BlockSpec((tk, dv), lambda b, h, i, j, t: (kb0 + b * kbs + j, kcol + h)),
            pl.BlockSpec((tk, dv), lambda b, h, i, j, t: (kb0 + b * kbs + j, vcol + h)),
            pl.BlockSpec((N_META, dv), lambda b, h, i, j, t: (mb0 + b, kcol + h)),
            pl.BlockSpec((N_META, dv), lambda b, h, i, j, t: (mb0 + b, vcol + h)),
            pl.BlockSpec((1, 1, tq, tk), lambda b, h, i, j, t: (h, t[i * nk + j], 0, 0)),
            pl.BlockSpec((1, 1, tq, LANES), lambda b, h, i, j, t: (h, jnp.minimum(i, n_bm - 1), 0, 0)),
            vec, vec, vec, vec,
            pl.BlockSpec((1, dv), lambda b, h, i, j, t: (0, 0)),
            pl.BlockSpec(memory_space=pl.ANY),
        ],
        out_specs=pl.BlockSpec((tq, dv), qmap),
        scratch_shapes=[pltpu.VMEM((tq, 1), F32), pltpu.VMEM((tq, 1), F32), pltpu.VMEM((tq, dv), F32),
                        pltpu.VMEM((tq, 1), F32), pltpu.VMEM((tq, 1), F32), pltpu.VMEM((tq, dv), F32)],
    )
    return pl.pallas_call(
        _diff_attn_kernel,
        grid_spec=grid_spec,
        out_shape=jax.ShapeDtypeStruct(oa.shape, oa.dtype),
        input_output_aliases={13: 0},
        compiler_params=_params(("parallel", "parallel", "parallel", "arbitrary")),
        name=f"diff_attn_q{tq}",
    )(jnp.asarray(ids), proj, proj, proj, proj, proj, tiles, biasm, *lam_vecs, subln, oa)


def _na_plan():
    rows_c = 3 * NA_ROWS
    a = np.arange(NA_ROWS)
    kl = np.arange(NA_WIN_ROWS)
    ridx = np.zeros((3, NA_ROWS, NA_WIN_ROWS), np.int32)
    rvalid = np.zeros((3, NA_ROWS, NA_WIN_ROWS), bool)
    for v in range(3):
        r = v * NA_ROWS + a
        rs = np.clip(r - NA_MAX_ROWS // 2, 0, rows_c - NA_MAX_ROWS)
        kr = v * NA_ROWS - NA_MAX_ROWS // 2 + kl
        off = kr[None, :] - r[:, None]
        ok = (kr[None, :] >= rs[:, None]) & (kr[None, :] < rs[:, None] + NA_MAX_ROWS)
        ok &= (kr[None, :] >= 0) & (kr[None, :] < rows_c)
        ridx[v] = np.clip(off + NA_MAX_ROWS - 1, 0, 2 * NA_MAX_ROWS - 2)
        rvalid[v] = ok
    c = np.arange(GRID_W)
    cs = np.clip(c - NA_COLS // 2, 0, GRID_W - NA_COLS)
    coff = c[None, :] - c[:, None]
    cvalid = (c[None, :] >= cs[:, None]) & (c[None, :] < cs[:, None] + NA_COLS)
    cidx = np.clip(coff + NA_COLS - 1, 0, 2 * NA_COLS - 2).astype(np.int32)
    return ridx, rvalid, cidx, cvalid


def _na_bias_tiles(rpb):
    ridx, rvalid, cidx, cvalid = _na_plan()
    h = rpb.shape[0]
    t1 = rpb[:, :, cidx]
    t2 = t1[:, ridx]
    valid = rvalid[:, :, :, None, None] & cvalid[None, None, None]
    t2 = jnp.where(jnp.asarray(valid)[None], t2 * LOG2E, NEG)
    t2 = jnp.transpose(t2, (0, 1, 2, 4, 3, 5))
    return t2.reshape(h, 3, NA_BLOCK, NA_WIN).astype(F32)


def _na_kernel(q_ref, k0_ref, k1_ref, k2_ref, k3_ref, v0_ref, v1_ref, v2_ref, v3_ref, km_ref, vm_ref,
               bias_ref, prev_ref, o_ref):
    del prev_ref
    q = q_ref[...]
    k = jnp.concatenate([k0_ref[...], k1_ref[...], k2_ref[...], k3_ref[...]], axis=0)
    v = jnp.concatenate([v0_ref[...], v1_ref[...], v2_ref[...], v3_ref[...]], axis=0)
    s = _qk(q, k) + bias_ref[0, 0]
    km = _pad_meta(km_ref[...])
    vm = _pad_meta(vm_ref[...])
    lane = lax.broadcasted_iota(I32, (q.shape[0], LANES), 1)
    sm = jnp.where(lane < N_META, _qk(q, km), NEG)
    m = jnp.maximum(jnp.max(s, axis=-1, keepdims=True), jnp.max(sm, axis=-1, keepdims=True))
    p = jnp.exp2(s - m)
    pm = jnp.exp2(sm - m)
    l = jnp.sum(p, axis=-1, keepdims=True) + jnp.sum(pm, axis=-1, keepdims=True)
    o = (jnp.dot(p.astype(v.dtype), v, preferred_element_type=F32)
         + jnp.dot(pm.astype(vm.dtype), vm, preferred_element_type=F32))
    o_ref[...] = (o / l).astype(o_ref.dtype)


def _na_attn(proj, ob, bias, *, w_a, w_b, n_seq, seq_len, row0, meta_row0):
    h_b = w_b // HD
    nblk = seq_len // NA_BLOCK
    assert seq_len % NA_BLOCK == 0 and nblk >= 2 and row0 % NA_BLOCK == 0
    rb0 = row0 // NA_BLOCK
    mb0 = meta_row0 // N_META
    qcol, kcol, vcol = 3 * w_a // HD, (3 * w_a + w_b) // HD, (3 * w_a + 2 * w_b) // HD

    per_q = NA_BLOCK // NA_KBLOCK
    n_kb = nblk * per_q

    def kblk(col, t):
        def index(h, b, i):
            return (rb0 * per_q + b * n_kb + jnp.clip(i * per_q - 1 + t, 0, n_kb - 1), col + h)
        return pl.BlockSpec((NA_KBLOCK, HD), index)

    def variant(i):
        return jnp.where(i == 0, 0, jnp.where(i == nblk - 1, 2, 1))

    n_win = NA_WIN // NA_KBLOCK
    return pl.pallas_call(
        _na_kernel,
        grid=(h_b, n_seq, nblk),
        in_specs=[pl.BlockSpec((NA_BLOCK, HD), lambda h, b, i: (rb0 + b * nblk + i, qcol + h))]
                 + [kblk(kcol, t) for t in range(n_win)] + [kblk(vcol, t) for t in range(n_win)]
                 + [pl.BlockSpec((N_META, HD), lambda h, b, i: (mb0 + b, kcol + h)),
                    pl.BlockSpec((N_META, HD), lambda h, b, i: (mb0 + b, vcol + h)),
                    pl.BlockSpec((1, 1, NA_BLOCK, NA_WIN), lambda h, b, i: (h, variant(i), 0, 0)),
                    pl.BlockSpec(memory_space=pl.ANY)],
        out_specs=pl.BlockSpec((NA_BLOCK, HD), lambda h, b, i: (rb0 + b * nblk + i, h)),
        out_shape=jax.ShapeDtypeStruct(ob.shape, ob.dtype),
        input_output_aliases={2 * n_win + 4: 0},
        compiler_params=_params(("parallel", "parallel", "parallel")),
        name="na_attn",
    )(*([proj] * (2 * n_win + 3)), bias, ob)


def _na_meta_kernel(q_ref, km_ref, vm_ref, prev_ref, o_ref):
    del prev_ref
    q = jnp.concatenate([q_ref[...], jnp.zeros((LANES - N_META, HD), q_ref.dtype)], axis=0)
    km = _pad_meta(km_ref[...])
    vm = _pad_meta(vm_ref[...])
    lane = lax.broadcasted_iota(I32, (LANES, LANES), 1)
    s = jnp.where(lane < N_META, _qk(q, km), NEG)
    m = jnp.max(s, axis=-1, keepdims=True)
    p = jnp.exp2(s - m)
    o = jnp.dot(p.astype(vm.dtype), vm, preferred_element_type=F32) / jnp.sum(p, axis=-1, keepdims=True)
    o_ref[...] = o[:N_META].astype(o_ref.dtype)


def _na_meta_attn(proj, ob, *, w_a, w_b, n_seq, meta_row0):
    h_b = w_b // HD
    mb0 = meta_row0 // N_META
    qcol, kcol, vcol = 3 * w_a // HD, (3 * w_a + w_b) // HD, (3 * w_a + 2 * w_b) // HD
    return pl.pallas_call(
        _na_meta_kernel,
        grid=(n_seq, h_b),
        in_specs=[pl.BlockSpec((N_META, HD), lambda s, h: (mb0 + s, qcol + h)),
                  pl.BlockSpec((N_META, HD), lambda s, h: (mb0 + s, kcol + h)),
                  pl.BlockSpec((N_META, HD), lambda s, h: (mb0 + s, vcol + h)),
                  pl.BlockSpec(memory_space=pl.ANY)],
        out_specs=pl.BlockSpec((N_META, HD), lambda s, h: (mb0 + s, h)),
        out_shape=jax.ShapeDtypeStruct(ob.shape, ob.dtype),
        input_output_aliases={3: 0},
        compiler_params=_params(("parallel", "parallel")),
        name="na_meta_attn",
    )(proj, proj, proj, ob)


def _merge_kernel(oa_ref, ob_ref, wa_ref, wb_ref, ga_ref, gb_ref, o_ref):
    a = jnp.dot(oa_ref[...], wa_ref[...], preferred_element_type=F32)
    b = jnp.dot(ob_ref[...], wb_ref[...], preferred_element_type=F32)
    o_ref[...] = (ga_ref[...].astype(F32) * a + gb_ref[...].astype(F32) * b).astype(o_ref.dtype)


def _merge(oa, ob, wa, wb, proj, gate_col0):
    m, w_a = oa.shape
    w_b = ob.shape[1]
    d = wa.shape[1]
    tm = min(ROW_TILE, m)
    tn = min(COL_TILE, d)
    ga0 = gate_col0 // tn
    gb0 = (gate_col0 + d) // tn
    return pl.pallas_call(
        _merge_kernel,
        grid=(d // tn, m // tm),
        in_specs=[pl.BlockSpec((tm, w_a), lambda j, i: (i, 0)),
                  pl.BlockSpec((tm, w_b), lambda j, i: (i, 0)),
                  pl.BlockSpec((w_a, tn), lambda j, i: (0, j)),
                  pl.BlockSpec((w_b, tn), lambda j, i: (0, j)),
                  pl.BlockSpec((tm, tn), lambda j, i: (i, ga0 + j)),
                  pl.BlockSpec((tm, tn), lambda j, i: (i, gb0 + j))],
        out_specs=pl.BlockSpec((tm, tn), lambda j, i: (i, j)),
        out_shape=jax.ShapeDtypeStruct((m, d), BF16),
        compiler_params=_params(("parallel", "parallel")),
        name="merge",
    )(oa, ob, wa, wb, proj, proj)


def _outproj_kernel(x_ref, w_ref, h_ref, o_ref):
    o_ref[...] = h_ref[...] + jnp.dot(x_ref[...], w_ref[...], preferred_element_type=F32)


def _outproj(merged, w, h):
    m, d = merged.shape
    n = w.shape[1]
    tm = min(ROW_TILE, m)
    tn = min(COL_TILE, n)
    return pl.pallas_call(
        _outproj_kernel,
        grid=(n // tn, m // tm),
        in_specs=[pl.BlockSpec((tm, d), lambda j, i: (i, 0)),
                  pl.BlockSpec((d, tn), lambda j, i: (0, j)),
                  pl.BlockSpec((tm, tn), lambda j, i: (i, j))],
        out_specs=pl.BlockSpec((tm, tn), lambda j, i: (i, j)),
        out_shape=jax.ShapeDtypeStruct((m, n), F32),
        compiler_params=_params(("parallel", "parallel")),
        name="outproj",
    )(merged, w, h)


def _pack_bf16_pairs(x):
    half = x.shape[1] // 2
    lo = lax.bitcast_convert_type(x[:, :half].astype(jnp.bfloat16).astype(F32), U32)
    hi = lax.bitcast_convert_type(x[:, half:].astype(jnp.bfloat16).astype(F32), U32)
    return (lo >> 16) | (hi & jnp.uint32(0xFFFF0000))


def _unpack_bf16_pairs(u):
    lo = lax.bitcast_convert_type(u << 16, F32).astype(BF16)
    hi = lax.bitcast_convert_type(u & jnp.uint32(0xFFFF0000), F32).astype(BF16)
    return jnp.concatenate([lo, hi], axis=1)


def _router_kernel(h_ref, g_ref, wr_ref, br_ref, hf_ref, ti_ref, tw_ref):
    x = h_ref[...]
    ms = jnp.mean(x * x, axis=-1, keepdims=True)
    hf = x * lax.rsqrt(ms + EPS) * g_ref[...]
    hf_ref[...] = _pack_bf16_pairs(hf)
    logits = jnp.dot(hf, wr_ref[...], preferred_element_type=F32,
                     precision=lax.Precision.HIGHEST) + br_ref[...]
    lane = lax.broadcasted_iota(I32, logits.shape, 1).astype(F32)
    vals, idxs = [], []
    for _ in range(TOP_K):
        mx = jnp.max(logits, axis=-1, keepdims=True)
        ix = jnp.min(jnp.where(logits == mx, lane, float(LANES)), axis=-1, keepdims=True)
        vals.append(mx)
        idxs.append(ix)
        logits = jnp.where(lane == ix, -jnp.inf, logits)
    es = [jnp.exp(v - vals[0]) for v in vals]
    tot = es[0]
    for e in es[1:]:
        tot = tot + e
    ti = jnp.zeros(lane.shape, F32)
    tw = jnp.zeros(lane.shape, F32)
    for k in range(TOP_K):
        ti = jnp.where(lane == k, idxs[k], ti)
        tw = jnp.where(lane == k, es[k] / tot, tw)
    ti_ref[...] = ti.astype(I32)
    tw_ref[...] = tw


def _router(h1, g, w_router, b_router):
    m, d = h1.shape
    e = w_router.shape[1]
    assert e <= LANES
    tm = min(NORM_TILE, m)
    wr = jnp.zeros((d, LANES), F32).at[:, :e].set(w_router.astype(F32))
    br = jnp.full((1, LANES), NEG, F32).at[0, :e].set(b_router.astype(F32))
    return pl.pallas_call(
        _router_kernel,
        grid=(m // tm,),
        in_specs=[pl.BlockSpec((tm, d), lambda i: (i, 0)),
                  pl.BlockSpec((1, d), lambda i: (0, 0)),
                  pl.BlockSpec((d, LANES), lambda i: (0, 0)),
                  pl.BlockSpec((1, LANES), lambda i: (0, 0))],
        out_specs=[pl.BlockSpec((tm, d // 2), lambda i: (i, 0)),
                   pl.BlockSpec((tm, LANES), lambda i: (i, 0)),
                   pl.BlockSpec((tm, LANES), lambda i: (i, 0))],
        out_shape=[jax.ShapeDtypeStruct((m, d // 2), U32),
                   jax.ShapeDtypeStruct((m, LANES), I32),
                   jax.ShapeDtypeStruct((m, LANES), F32)],
        compiler_params=_params(("parallel",)),
        name="router",
    )(h1, g.reshape(1, d).astype(F32), wr, br)


def _row_copy(src_ref, src_row, dst_ref, dst_row, sem):
    return pltpu.make_async_copy(src_ref.at[pl.ds(src_row, 1)], dst_ref.at[pl.ds(dst_row, 1)], sem)


def _gather_kernel(nu_ref, idx_ref, src_ref, o_ref, buf, sem):
    rows = buf.shape[0]

    def issue(r, c):
        _row_copy(src_ref, idx_ref[0, r], buf, r, sem).start()
        return c

    def drain(r, c):
        _row_copy(src_ref, 0, buf, r, sem).wait()
        return c

    @pl.when(pl.program_id(0) < nu_ref[0])
    def _():
        lax.fori_loop(0, rows, issue, 0, unroll=DMA_UNROLL)
        lax.fori_loop(0, rows, drain, 0, unroll=DMA_UNROLL)
        o_ref[...] = _unpack_bf16_pairs(buf[...])

    @pl.when(pl.program_id(0) >= nu_ref[0])
    def _():
        o_ref[...] = jnp.zeros(o_ref.shape, o_ref.dtype)


def _gather_rows(src, slot_rows, n_used, tile):
    n_slots = slot_rows.shape[0]
    half = src.shape[1]
    nb = n_slots // tile
    return pl.pallas_call(
        _gather_kernel,
        grid=(nb,),
        in_specs=[pl.BlockSpec(memory_space=pltpu.SMEM),
                  pl.BlockSpec((None, 1, tile), lambda b: (b, 0, 0), memory_space=pltpu.SMEM),
                  pl.BlockSpec(memory_space=pl.ANY)],
        out_specs=pl.BlockSpec((tile, 2 * half), lambda b: (b, 0)),
        out_shape=jax.ShapeDtypeStruct((n_slots, 2 * half), BF16),
        scratch_shapes=[pltpu.VMEM((tile, half), src.dtype), pltpu.SemaphoreType.DMA(())],
        compiler_params=_params(("arbitrary",)),
        name="moe_gather",
    )(n_used, slot_rows.reshape(nb, 1, tile), src)


def _swiglu(x, wg, wu, bg, bu):
    g = jnp.dot(x, wg, preferred_element_type=F32) + bg
    u = jnp.dot(x, wu, preferred_element_type=F32) + bu
    g = jnp.minimum(g, SWIGLU_LIMIT)
    u = jnp.clip(u, -SWIGLU_LIMIT, SWIGLU_LIMIT)
    return ((u + 1.0) * (g * (1.0 / (1.0 + jnp.exp(-SWIGLU_ALPHA * g))))).astype(BF16)


def _moe_up_kernel(be_ref, nu_ref, nv_ref, x_ref, wg_ref, wu_ref, bg_ref, bu_ref, o_ref):
    del be_ref
    b = pl.program_id(0)
    half = x_ref.shape[0] // 2

    @pl.when(b < nu_ref[0])
    def _():
        wg = wg_ref[0].astype(BF16)
        wu = wu_ref[0].astype(BF16)

        @pl.when(nv_ref[b] > half)
        def _():
            o_ref[...] = _swiglu(x_ref[...], wg, wu, bg_ref[0], bu_ref[0])

        @pl.when(nv_ref[b] <= half)
        def _():
            o_ref[:half] = _swiglu(x_ref[:half], wg, wu, bg_ref[0], bu_ref[0])
            o_ref[half:] = jnp.zeros((half, o_ref.shape[1]), o_ref.dtype)

    @pl.when(b >= nu_ref[0])
    def _():
        o_ref[...] = jnp.zeros(o_ref.shape, o_ref.dtype)


def _moe_down_kernel(be_ref, nu_ref, nv_ref, a_ref, wd_ref, bd_ref, o_ref):
    del be_ref
    b = pl.program_id(0)
    half = a_ref.shape[0] // 2

    @pl.when(b < nu_ref[0])
    def _():
        wd = wd_ref[0].astype(BF16)

        @pl.when(nv_ref[b] > half)
        def _():
            o_ref[...] = jnp.dot(a_ref[...], wd, preferred_element_type=F32) + bd_ref[0]

        @pl.when(nv_ref[b] <= half)
        def _():
            o_ref[:half] = jnp.dot(a_ref[:half], wd, preferred_element_type=F32) + bd_ref[0]
            o_ref[half:] = jnp.zeros((half, o_ref.shape[1]), o_ref.dtype)

    @pl.when(b >= nu_ref[0])
    def _():
        o_ref[...] = jnp.zeros(o_ref.shape, o_ref.dtype)


def _moe_experts(xs, block_e, n_used, n_valid, wg, wu, wd, bg, bu, bd, tile):
    n_slots, d = xs.shape
    e, _, ff = wg.shape
    tf = min(MOE_FF_TILE, ff)
    tn = min(MOE_DOWN_TILE, d)
    nb, nf, nn = n_slots // tile, ff // tf, d // tn

    def bb(b, nu):
        return jnp.minimum(b, nu[0] - 1)

    def frozen(b, j, last, nu):
        return jnp.where(b < nu[0], j, last)

    up_spec = pltpu.PrefetchScalarGridSpec(
        num_scalar_prefetch=3,
        grid=(nb, nf),
        in_specs=[
            pl.BlockSpec((tile, d), lambda b, f, be, nu, nv: (bb(b, nu), 0)),
            pl.BlockSpec((1, d, tf), lambda b, f, be, nu, nv: (be[bb(b, nu)], 0, frozen(b, f, nf - 1, nu))),
            pl.BlockSpec((1, d, tf), lambda b, f, be, nu, nv: (be[bb(b, nu)], 0, frozen(b, f, nf - 1, nu))),
            pl.BlockSpec((1, 1, tf), lambda b, f, be, nu, nv: (be[bb(b, nu)], 0, frozen(b, f, nf - 1, nu))),
            pl.BlockSpec((1, 1, tf), lambda b, f, be, nu, nv: (be[bb(b, nu)], 0, frozen(b, f, nf - 1, nu))),
        ],
        out_specs=pl.BlockSpec((tile, tf), lambda b, f, be, nu, nv: (b, f)),
    )
    act = pl.pallas_call(
        _moe_up_kernel,
        grid_spec=up_spec,
        out_shape=jax.ShapeDtypeStruct((n_slots, ff), BF16),
        compiler_params=_params(("arbitrary", "arbitrary")),
        name="moe_up",
    )(block_e, n_used, n_valid, xs, wg, wu, bg.reshape(e, 1, ff), bu.reshape(e, 1, ff))

    down_spec = pltpu.PrefetchScalarGridSpec(
        num_scalar_prefetch=3,
        grid=(nb, nn),
        in_specs=[
            pl.BlockSpec((tile, ff), lambda b, n, be, nu, nv: (bb(b, nu), 0)),
            pl.BlockSpec((1, ff, tn), lambda b, n, be, nu, nv: (be[bb(b, nu)], 0, frozen(b, n, nn - 1, nu))),
            pl.BlockSpec((1, 1, tn), lambda b, n, be, nu, nv: (be[bb(b, nu)], 0, frozen(b, n, nn - 1, nu))),
        ],
        out_specs=pl.BlockSpec((tile, tn), lambda b, n, be, nu, nv: (b, n)),
    )
    return pl.pallas_call(
        _moe_down_kernel,
        grid_spec=down_spec,
        out_shape=jax.ShapeDtypeStruct((n_slots, d), F32),
        compiler_params=_params(("arbitrary", "arbitrary")),
        name="moe_down",
    )(block_e, n_used, n_valid, act, wd, bd.reshape(e, 1, d))


def _combine_kernel(pos_ref, w_ref, h_ref, ys_ref, o_ref, buf, sem):
    rows = o_ref.shape[0]

    def issue(r, c):
        for k in range(TOP_K):
            _row_copy(ys_ref, pos_ref[0, TOP_K * r + k], buf.at[k], r, sem).start()
        return c

    def drain(r, c):
        for k in range(TOP_K):
            _row_copy(ys_ref, 0, buf.at[k], r, sem).wait()
        return c

    lax.fori_loop(0, rows, issue, 0, unroll=DMA_UNROLL // 2)
    lax.fori_loop(0, rows, drain, 0, unroll=DMA_UNROLL // 2)
    w = w_ref[...]
    acc = h_ref[...]
    for k in range(TOP_K):
        acc = acc + w[:, k:k + 1] * buf[k]
    o_ref[...] = acc


def _combine(h1, tw, pos, ys, row0, n_rows):
    d = h1.shape[1]
    tc = min(COMBINE_TILE, n_rows)
    assert n_rows % tc == 0 and row0 % tc == 0
    nb, b0 = n_rows // tc, row0 // tc
    pos_blocks = pos[row0:row0 + n_rows].reshape(nb, 1, tc * TOP_K)
    return pl.pallas_call(
        _combine_kernel,
        grid=(nb,),
        in_specs=[pl.BlockSpec((None, 1, tc * TOP_K), lambda i: (i, 0, 0), memory_space=pltpu.SMEM),
                  pl.BlockSpec((tc, LANES), lambda i: (b0 + i, 0)),
                  pl.BlockSpec((tc, d), lambda i: (b0 + i, 0)),
                  pl.BlockSpec(memory_space=pl.ANY)],
        out_specs=pl.BlockSpec((tc, d), lambda i: (i, 0)),
        out_shape=jax.ShapeDtypeStruct((n_rows, d), F32),
        scratch_shapes=[pltpu.VMEM((TOP_K, tc, d), F32), pltpu.SemaphoreType.DMA(())],
        compiler_params=_params(("arbitrary",)),
        name="moe_combine",
    )(pos_blocks, tw, h1, ys)


def _dispatch_plan(top_i, n_experts, tile):
    n_tok = top_i.shape[0]
    nk = n_tok * TOP_K
    flat_e = top_i.reshape(-1)
    onehot = (flat_e[:, None] == jnp.arange(n_experts, dtype=I32)[None, :]).astype(I32)
    rank = jnp.take_along_axis(jnp.cumsum(onehot, axis=0), flat_e[:, None], axis=1)[:, 0] - 1
    counts = jnp.sum(onehot, axis=0)
    nblk = (counts + tile - 1) // tile
    blk_end = jnp.cumsum(nblk)
    blk_start = blk_end - nblk
    pos = blk_start[flat_e] * tile + rank
    nb_max = -(-nk // tile) + n_experts
    block_e = jnp.clip(jnp.searchsorted(blk_end, jnp.arange(nb_max, dtype=I32), side="right"),
                       0, n_experts - 1).astype(I32)
    n_used = blk_end[-1:].astype(I32)
    blocks = jnp.arange(nb_max, dtype=I32)
    n_valid = jnp.clip(counts[block_e] - (blocks - blk_start[block_e]) * tile, 0, tile)
    n_valid = jnp.where(blocks < n_used[0], n_valid, 0).astype(I32)
    return pos.astype(I32), block_e, n_used, n_valid, nb_max


def _round_up(x, m):
    return -(-x // m) * m


def kernel(x_prompt, x_sample, meta_tokens, norm_mix, w_in, qk_norm_a_q, qk_norm_a_k, lambda_q1, lambda_k1, lambda_q2, lambda_k2, subln_a, rel_bias, qk_norm_b_q, qk_norm_b_k, na_rpb, w_br_a, w_br_b, w_out, norm_ffn, w_router, b_router, w_gate, b_gate, w_up, b_up, w_down, b_down):
    d = x_prompt.shape[-1]
    w_a = w_br_a.shape[1]
    w_b = w_br_b.shape[1]
    n_experts = w_router.shape[-1]
    groups = [x_prompt, x_sample]

    row0s, seqs = [], []
    m_real = 0
    for x in groups:
        row0s.append(m_real)
        m_real += x.shape[0] * x.shape[1]
    n_seq = sum(x.shape[0] for x in groups)
    meta_row0 = m_real
    m_tok = m_real + n_seq * N_META
    m_pad = _round_up(m_tok + 1, ROW_TILE)
    h_all = jnp.concatenate(
        [x.reshape(-1, d) for x in groups]
        + [jnp.tile(meta_tokens.astype(F32), (n_seq, 1)), jnp.zeros((m_pad - m_tok, d), F32)], axis=0)

    hn = _rmsnorm_rows(h_all, norm_mix[0], BF16)
    scale_a, scale_b = HD ** -0.5 * LOG2E, HD ** -0.5 * LOG2E
    ones = lambda n: jnp.ones((n,), F32)
    gain = jnp.concatenate([
        jnp.tile(qk_norm_a_q[0].astype(F32) * scale_a, w_a // HD), jnp.tile(qk_norm_a_k[0].astype(F32), w_a // HD),
        ones(w_a),
        jnp.tile(qk_norm_b_q[0].astype(F32) * scale_b, w_b // HD), jnp.tile(qk_norm_b_k[0].astype(F32), w_b // HD),
        ones(w_b + 2 * d)]).reshape(1, -1)
    proj = _inproj(hn, w_in[0].astype(BF16), gain, w_a, w_b)

    lam_vecs = [v[0].reshape(1, HD).astype(F32) for v in (lambda_q1, lambda_k1, lambda_q2, lambda_k2)]
    subln = subln_a[0].reshape(1, 2 * HD).astype(F32)
    na_bias = _na_bias_tiles(na_rpb[0].astype(F32))
    oa = jnp.zeros((m_pad, w_a), BF16)
    ob = jnp.zeros((m_pad, w_b), BF16)
    tile_cache = {}
    seq0 = 0
    for x, row0 in zip(groups, row0s):
        nb, s = x.shape[0], x.shape[1]
        mrow = meta_row0 + seq0 * N_META
        tq = min(ATTN_TQ, s)
        common = dict(w_a=w_a, n_seq=nb, seq_len=s, kv_row0=row0, meta_row0=mrow, tk=min(ATTN_TK, s))
        oa = _diff_attn(proj, oa, rel_bias.astype(F32), lam_vecs, subln, tile_cache, q_row0=row0,
                        q_stride=s, tq=tq, nq=s // tq, qpos0=N_META, **common)
        oa = _diff_attn(proj, oa, rel_bias.astype(F32), lam_vecs, subln, tile_cache, q_row0=mrow,
                        q_stride=N_META, tq=N_META, nq=1, qpos0=0, **common)
        ob = _na_attn(proj, ob, na_bias, w_a=w_a, w_b=w_b, n_seq=nb, seq_len=s, row0=row0, meta_row0=mrow)
        seq0 += nb
    ob = _na_meta_attn(proj, ob, w_a=w_a, w_b=w_b, n_seq=n_seq, meta_row0=meta_row0)

    merged = _merge(oa, ob, w_br_a[0].astype(BF16), w_br_b[0].astype(BF16), proj, 3 * w_a + 3 * w_b)
    h1 = _outproj(merged, w_out[0].astype(BF16), h_all)

    hf, top_i, top_w = _router(h1, norm_ffn[0], w_router[0], b_router[0])
    pos, block_e, n_used, n_valid, nb_max = _dispatch_plan(top_i[:m_tok, :TOP_K], n_experts, MOE_TILE)
    slot_rows = jnp.full((nb_max * MOE_TILE,), m_tok, I32).at[pos].set(
        jnp.repeat(jnp.arange(m_tok, dtype=I32), TOP_K))
    xs = _gather_rows(hf, slot_rows, n_used, MOE_TILE)
    ys = _moe_experts(xs, block_e, n_used, n_valid, w_gate[0].astype(F32), w_up[0].astype(F32),
                      w_down[0].astype(F32), b_gate[0].astype(F32), b_up[0].astype(F32),
                      b_down[0].astype(F32), MOE_TILE)
    pos2 = pos.reshape(m_tok, TOP_K)
    outs = []
    for x, row0 in zip(groups, row0s):
        n_rows = x.shape[0] * x.shape[1]
        outs.append(_combine(h1, top_w, pos2, ys, row0, n_rows).reshape(x.shape))
    return tuple(outs)
```

```python
import functools
import math

import numpy as np
import jax
import jax.numpy as jnp
from jax import lax
from jax.experimental import pallas as pl
from jax.experimental.pallas import tpu as pltpu

F32 = jnp.float32
BF16 = jnp.bfloat16
U32 = jnp.uint32
I32 = jnp.int32

N_META = 16
GRID_W = 64
HD = 128
REL_BUCKETS = 32
REL_MAX_DIST = 128
NA_MAX_ROWS = 8
NA_COLS = 16
TOP_K = 4
SWIGLU_LIMIT = 7.0
SWIGLU_ALPHA = 1.702
EPS = 1e-6
LAMBDA_INIT = 0.8 - 0.6 * math.exp(-0.3 * 0)
NEG = -1e30
LOG2E = math.log2(math.e)

LANES = 128
VMEM_LIMIT_MB = 56
ROW_TILE = 1024
COL_TILE = 1024
NORM_TILE = 256
NA_ROWS = 8
NA_BLOCK = NA_ROWS * GRID_W
NA_WIN_ROWS = NA_ROWS + NA_MAX_ROWS
NA_WIN = NA_WIN_ROWS * GRID_W
NA_KBLOCK = NA_WIN // 4
MOE_TILE = 1024
MOE_FF_TILE = 256
MOE_DOWN_TILE = 512
ATTN_TQ = 1024
ATTN_TK = 2048
COMBINE_TILE = 128
DMA_UNROLL = 8


def _params(semantics, vmem_mb=VMEM_LIMIT_MB):
    return pltpu.CompilerParams(dimension_semantics=semantics,
                                vmem_limit_bytes=vmem_mb * 2**20)


def _rmsnorm_kernel(x_ref, g_ref, o_ref):
    x = x_ref[...].astype(F32)
    ms = jnp.mean(x * x, axis=-1, keepdims=True)
    o_ref[...] = (x * lax.rsqrt(ms + EPS) * g_ref[...]).astype(o_ref.dtype)


def _rmsnorm_rows(x, g, out_dtype):
    m, d = x.shape
    tm = min(NORM_TILE, m)
    return pl.pallas_call(
        _rmsnorm_kernel,
        grid=(m // tm,),
        in_specs=[pl.BlockSpec((tm, d), lambda i: (i, 0)),
                  pl.BlockSpec((1, d), lambda i: (0, 0))],
        out_specs=pl.BlockSpec((tm, d), lambda i: (i, 0)),
        out_shape=jax.ShapeDtypeStruct((m, d), out_dtype),
        compiler_params=_params(("parallel",)),
        name="rmsnorm_rows",
    )(x, g.reshape(1, d).astype(F32))


def _inproj_kernel(x_ref, w_ref, gain_ref, o_ref, *, norm_ranges, gate_start):
    j = pl.program_id(0)
    acc = jnp.dot(x_ref[...], w_ref[...], preferred_element_type=F32)
    tn = acc.shape[1]

    is_norm = False
    for lo, hi in norm_ranges:
        is_norm = jnp.logical_or(is_norm, jnp.logical_and(j >= lo, j < hi))
    is_gate = j >= gate_start

    @pl.when(is_norm)
    def _():
        for c in range(tn // HD):
            blk = acc[:, c * HD:(c + 1) * HD]
            ms = jnp.mean(blk * blk, axis=-1, keepdims=True)
            o_ref[:, c * HD:(c + 1) * HD] = (
                blk * lax.rsqrt(ms + EPS) * gain_ref[:, c * HD:(c + 1) * HD]).astype(o_ref.dtype)

    @pl.when(is_gate)
    def _():
        o_ref[...] = (1.0 / (1.0 + jnp.exp(-acc))).astype(o_ref.dtype)

    @pl.when(jnp.logical_not(jnp.logical_or(is_norm, is_gate)))
    def _():
        o_ref[...] = acc.astype(o_ref.dtype)


def _inproj(hn, w, gain, w_a, w_b):
    m, d = hn.shape
    n = w.shape[1]
    tm = min(ROW_TILE, m)
    tn = min(COL_TILE, w_a, w_b)
    assert w_a % tn == 0 and w_b % tn == 0 and n % tn == 0 and m % tm == 0
    norm_ranges = ((0, 2 * w_a // tn), (3 * w_a // tn, (3 * w_a + 2 * w_b) // tn))
    gate_start = (3 * w_a + 3 * w_b) // tn
    kern = functools.partial(_inproj_kernel, norm_ranges=norm_ranges, gate_start=gate_start)
    return pl.pallas_call(
        kern,
        grid=(n // tn, m // tm),
        in_specs=[pl.BlockSpec((tm, d), lambda j, i: (i, 0)),
                  pl.BlockSpec((d, tn), lambda j, i: (0, j)),
                  pl.BlockSpec((1, tn), lambda j, i: (0, j))],
        out_specs=pl.BlockSpec((tm, tn), lambda j, i: (i, j)),
        out_shape=jax.ShapeDtypeStruct((m, n), BF16),
        compiler_params=_params(("parallel", "parallel")),
        name="inproj",
    )(hn, w, gain)


def _t5_bucket(rel):
    half = REL_BUCKETS // 2
    max_exact = half // 2
    sign = (rel > 0).astype(I32) * half
    n = jnp.abs(rel)
    nf = jnp.maximum(n, 1).astype(F32)
    large = max_exact + (jnp.log(nf / max_exact) / math.log(REL_MAX_DIST / max_exact)
                         * (half - max_exact)).astype(I32)
    large = jnp.minimum(large, half - 1)
    return sign + jnp.where(n < max_exact, n, large)


def _bias_tile_plan(qpos0, nq, tq, nk, tk):
    keys, ids = [], np.zeros((nq, nk), np.int32)
    for i in range(nq):
        for j in range(nk):
            d = (N_META + j * tk) - (qpos0 + i * tq)
            if d - (tq - 1) >= REL_MAX_DIST:
                key = ("far", REL_MAX_DIST)
            elif d + (tk - 1) <= -REL_MAX_DIST:
                key = ("far", -REL_MAX_DIST)
            else:
                key = ("near", d)
            if key not in keys:
                keys.append(key)
            ids[i, j] = keys.index(key)
    return tuple(keys), ids.reshape(-1)


def _bias_tiles(rel_bias, keys, tq, tk):
    h = rel_bias.shape[1]
    blocked = tq % LANES == 0 and tk % LANES == 0 and all(v % LANES == 0 for _, v in keys)
    if not blocked:
        ramp = np.arange(tk)[None, :] - np.arange(tq)[:, None]
        rel = np.stack([np.full((tq, tk), v) if kind == "far" else v + ramp for kind, v in keys])
        b = rel_bias[_t5_bucket(jnp.asarray(rel.astype(np.int32)))]
        return jnp.transpose(b, (3, 0, 1, 2)).astype(F32)

    def const(v):
        return rel_bias[_t5_bucket(jnp.full((1,), v, I32))].reshape(h, 1, 1, 1, 1)

    r = np.arange(LANES)
    rel3 = np.stack([LANES * dl + r[None, :] - r[:, None] for dl in (-1, 0, 1)]).astype(np.int32)
    f3 = jnp.transpose(rel_bias[_t5_bucket(jnp.asarray(rel3))], (3, 0, 1, 2))
    na, nc = tq // LANES, tk // LANES
    tiles = []
    for kind, v in keys:
        if kind == "far":
            tiles.append(jnp.broadcast_to(const(v).reshape(h, 1, 1), (h, tq, tk)))
            continue
        delta = v // LANES + np.arange(nc)[None, :] - np.arange(na)[:, None]
        t5 = jnp.where(jnp.asarray(delta < 0)[None, :, None, :, None], const(-REL_MAX_DIST), const(REL_MAX_DIST))
        for dl in (-1, 0, 1):
            t5 = jnp.where(jnp.asarray(delta == dl)[None, :, None, :, None], f3[:, dl + 1][:, None, :, None, :], t5)
        tiles.append(t5.reshape(h, tq, tk))
    return jnp.stack(tiles, axis=1).astype(F32)


def _meta_bias_tiles(rel_bias, qpos0, nq, tq):
    n = 1
    while n < nq and qpos0 + (n - 1) * tq - (N_META - 1) < REL_MAX_DIST:
        n += 1
    qpos = qpos0 + np.arange(n * tq).reshape(n, tq)
    rel = np.arange(N_META)[None, None, :] - qpos[:, :, None]
    b = jnp.transpose(rel_bias[_t5_bucket(jnp.asarray(rel.astype(np.int32)))], (3, 0, 1, 2)).astype(F32)
    pad = jnp.full(b.shape[:3] + (LANES - N_META,), NEG, F32)
    return jnp.concatenate([b, pad], axis=-1), n


def _softmax_step(s, m_ref, l_ref, a_ref, v, first):
    m_cur = jnp.max(s, axis=-1, keepdims=True)
    if first:
        m_new = m_cur
        p = jnp.exp2(s - m_new)
        l_ref[...] = jnp.sum(p, axis=-1, keepdims=True)
        a_ref[...] = jnp.dot(p.astype(v.dtype), v, preferred_element_type=F32)
    else:
        m_prev = m_ref[...]
        m_new = jnp.maximum(m_prev, m_cur)
        alpha = jnp.exp2(m_prev - m_new)
        p = jnp.exp2(s - m_new)
        l_ref[...] = alpha * l_ref[...] + jnp.sum(p, axis=-1, keepdims=True)
        a_ref[...] = alpha * a_ref[...] + jnp.dot(p.astype(v.dtype), v, preferred_element_type=F32)
    m_ref[...] = m_new


def _qk(q, k):
    return lax.dot_general(q, k, (((1,), (1,)), ((), ())), preferred_element_type=F32)


def _pad_meta(x):
    return jnp.concatenate([x, jnp.zeros((LANES - N_META, x.shape[1]), x.dtype)], axis=0)


def _diff_attn_kernel(tid_ref, q_ref, k_ref, v_ref, km_ref, vm_ref, bias_ref, biasm_ref,
                      lq1_ref, lk1_ref, lq2_ref, lk2_ref, sub_ref, prev_ref, o_ref,
                      m1, l1, a1, m2, l2, a2, *, nk):
    del tid_ref, prev_ref
    j = pl.program_id(3)
    q = q_ref[...]
    q1, q2 = q[:, :HD], q[:, HD:]

    @pl.when(j == 0)
    def _():
        km = _pad_meta(km_ref[...])
        vm = _pad_meta(vm_ref[...])
        bm = biasm_ref[0, 0]
        _softmax_step(_qk(q1, km[:, :HD]) + bm, m1, l1, a1, vm, True)
        _softmax_step(_qk(q2, km[:, HD:]) + bm, m2, l2, a2, vm, True)

    k = k_ref[...]
    v = v_ref[...]
    b = bias_ref[0, 0]
    _softmax_step(_qk(q1, k[:, :HD]) + b, m1, l1, a1, v, False)
    _softmax_step(_qk(q2, k[:, HD:]) + b, m2, l2, a2, v, False)

    @pl.when(j == nk - 1)
    def _():
        lam = (jnp.exp(jnp.sum(lq1_ref[...] * lk1_ref[...], axis=-1, keepdims=True))
               - jnp.exp(jnp.sum(lq2_ref[...] * lk2_ref[...], axis=-1, keepdims=True))
               + LAMBDA_INIT)
        o = a1[...] / l1[...] - lam * (a2[...] / l2[...])
        ms = jnp.mean(o * o, axis=-1, keepdims=True)
        o_ref[...] = (o * lax.rsqrt(ms + EPS) * sub_ref[...] * (1.0 - LAMBDA_INIT)).astype(o_ref.dtype)


def _diff_attn(proj, oa, rel_bias, lam_vecs, subln, tile_cache, *, w_a, n_seq, seq_len, kv_row0,
               q_row0, q_stride, tq, nq, qpos0, meta_row0, tk):
    h_a = w_a // (2 * HD)
    nk = seq_len // tk
    assert seq_len % tk == 0 and kv_row0 % tk == 0 and q_row0 % tq == 0 and q_stride % tq == 0
    assert meta_row0 % N_META == 0
    keys, ids = _bias_tile_plan(qpos0, nq, tq, nk, tk)
    if (keys, tq, tk) not in tile_cache:
        tile_cache[(keys, tq, tk)] = _bias_tiles(rel_bias, keys, tq, tk)
    tiles = tile_cache[(keys, tq, tk)]
    biasm, n_bm = _meta_bias_tiles(rel_bias, qpos0, nq, tq)
    qb0, qbs = q_row0 // tq, q_stride // tq
    kb0, kbs = kv_row0 // tk, seq_len // tk
    mb0 = meta_row0 // N_META
    kcol, vcol = w_a // (2 * HD), 2 * w_a // (2 * HD)
    dv = 2 * HD

    def qmap(b, h, i, j, t):
        return (qb0 + b * qbs + i, h)

    vec = pl.BlockSpec((1, HD), lambda b, h, i, j, t: (0, 0))
    grid_spec = pltpu.PrefetchScalarGridSpec(
        num_scalar_prefetch=1,
        grid=(n_seq, h_a, nq, nk),
        in_specs=[
            pl.BlockSpec((tq, dv), qmap),
            pl.BlockSpec((tk, dv), lambda b, h, i, j, t: (kb0 + b * kbs + j, kcol + h)),
            pl.BlockSpec((tk, dv), lambda b, h, i, j, t: (kb0 + b * kbs + j, vcol + h)),
            pl.BlockSpec((N_META, dv), lambda b, h, i, j, t: (mb0 + b, kcol + h)),
            pl.BlockSpec((N_META, dv), lambda b, h, i, j, t: (mb0 + b, vcol + h)),
            pl.BlockSpec((1, 1, tq, tk), lambda b, h, i, j, t: (h, t[i * nk + j], 0, 0)),
            pl.BlockSpec((1, 1, tq, LANES), lambda b, h, i, j, t: (h, jnp.minimum(i, n_bm - 1), 0, 0)),
            vec, vec, vec, vec,
            pl.BlockSpec((1, dv), lambda b, h, i, j, t: (0, 0)),
            pl.BlockSpec(memory_space=pl.ANY),
        ],
        out_specs=pl.BlockSpec((tq, dv), qmap),
        scratch_shapes=[pltpu.VMEM((tq, 1), F32), pltpu.VMEM((tq, 1), F32), pltpu.VMEM((tq, dv), F32),
                        pltpu.VMEM((tq, 1), F32), pltpu.VMEM((tq, 1), F32), pltpu.VMEM((tq, dv), F32)],
    )
    return pl.pallas_call(
        functools.partial(_diff_attn_kernel, nk=nk),
        grid_spec=grid_spec,
        out_shape=jax.ShapeDtypeStruct(oa.shape, oa.dtype),
        input_output_aliases={13: 0},
        compiler_params=_params(("parallel", "parallel", "parallel", "arbitrary")),
        name=f"diff_attn_q{tq}",
    )(jnp.asarray(ids), proj, proj, proj, proj, proj, tiles, biasm, *lam_vecs, subln, oa)


def _na_plan():
    rows_c = 3 * NA_ROWS
    a = np.arange(NA_ROWS)
    kl = np.arange(NA_WIN_ROWS)
    ridx = np.zeros((3, NA_ROWS, NA_WIN_ROWS), np.int32)
    rvalid = np.zeros((3, NA_ROWS, NA_WIN_ROWS), bool)
    for v in range(3):
        r = v * NA_ROWS + a
        rs = np.clip(r - NA_MAX_ROWS // 2, 0, rows_c - NA_MAX_ROWS)
        kr = v * NA_ROWS - NA_MAX_ROWS // 2 + kl
        off = kr[None, :] - r[:, None]
        ok = (kr[None, :] >= rs[:, None]) & (kr[None, :] < rs[:, None] + NA_MAX_ROWS)
        ok &= (kr[None, :] >= 0) & (kr[None, :] < rows_c)
        ridx[v] = np.clip(off + NA_MAX_ROWS - 1, 0, 2 * NA_MAX_ROWS - 2)
        rvalid[v] = ok
    c = np.arange(GRID_W)
    cs = np.clip(c - NA_COLS // 2, 0, GRID_W - NA_COLS)
    coff = c[None, :] - c[:, None]
    cvalid = (c[None, :] >= cs[:, None]) & (c[None, :] < cs[:, None] + NA_COLS)
    cidx = np.clip(coff + NA_COLS - 1, 0, 2 * NA_COLS - 2).astype(np.int32)
    return ridx, rvalid, cidx, cvalid


def _na_bias_tiles(rpb):
    ridx, rvalid, cidx, cvalid = _na_plan()
    h = rpb.shape[0]
    t1 = rpb[:, :, cidx]
    t2 = t1[:, ridx]
    valid = rvalid[:, :, :, None, None] & cvalid[None, None, None]
    t2 = jnp.where(jnp.asarray(valid)[None], t2, NEG)
    t2 = jnp.transpose(t2, (0, 1, 2, 4, 3, 5))
    return t2.reshape(h, 3, NA_BLOCK, NA_WIN).astype(F32)


def _na_kernel(q_ref, k0_ref, k1_ref, k2_ref, k3_ref, v0_ref, v1_ref, v2_ref, v3_ref, km_ref, vm_ref,
               bias_ref, prev_ref, o_ref):
    del prev_ref
    q = q_ref[...]
    k = jnp.concatenate([k0_ref[...], k1_ref[...], k2_ref[...], k3_ref[...]], axis=0)
    v = jnp.concatenate([v0_ref[...], v1_ref[...], v2_ref[...], v3_ref[...]], axis=0)
    s = _qk(q, k) + bias_ref[0, 0]
    km = _pad_meta(km_ref[...])
    vm = _pad_meta(vm_ref[...])
    lane = lax.broadcasted_iota(I32, (q.shape[0], LANES), 1)
    sm = jnp.where(lane < N_META, _qk(q, km), NEG)
    m = jnp.maximum(jnp.max(s, axis=-1, keepdims=True), jnp.max(sm, axis=-1, keepdims=True))
    p = jnp.exp2(s - m)
    pm = jnp.exp2(sm - m)
    l = jnp.sum(p, axis=-1, keepdims=True) + jnp.sum(pm, axis=-1, keepdims=True)
    o = (jnp.dot(p.astype(v.dtype), v, preferred_element_type=F32)
         + jnp.dot(pm.astype(vm.dtype), vm, preferred_element_type=F32))
    o_ref[...] = (o / l).astype(o_ref.dtype)


def _na_attn(proj, ob, bias, *, w_a, w_b, n_seq, seq_len, row0, meta_row0):
    h_b = w_b // HD
    nblk = seq_len // NA_BLOCK
    assert seq_len % NA_BLOCK == 0 and nblk >= 2 and row0 % NA_BLOCK == 0
    rb0 = row0 // NA_BLOCK
    mb0 = meta_row0 // N_META
    qcol, kcol, vcol = 3 * w_a // HD, (3 * w_a + w_b) // HD, (3 * w_a + 2 * w_b) // HD

    per_q = NA_BLOCK // NA_KBLOCK
    n_kb = nblk * per_q

    def kblk(col, t):
        def index(h, b, i):
            return (rb0 * per_q + b * n_kb + jnp.clip(i * per_q - 1 + t, 0, n_kb - 1), col + h)
        return pl.BlockSpec((NA_KBLOCK, HD), index)

    def variant(i):
        return jnp.where(i == 0, 0, jnp.where(i == nblk - 1, 2, 1))

    n_win = NA_WIN // NA_KBLOCK
    return pl.pallas_call(
        _na_kernel,
        grid=(h_b, n_seq, nblk),
        in_specs=[pl.BlockSpec((NA_BLOCK, HD), lambda h, b, i: (rb0 + b * nblk + i, qcol + h))]
                 + [kblk(kcol, t) for t in range(n_win)] + [kblk(vcol, t) for t in range(n_win)]
                 + [pl.BlockSpec((N_META, HD), lambda h, b, i: (mb0 + b, kcol + h)),
                    pl.BlockSpec((N_META, HD), lambda h, b, i: (mb0 + b, vcol + h)),
                    pl.BlockSpec((1, 1, NA_BLOCK, NA_WIN), lambda h, b, i: (h, variant(i), 0, 0)),
                    pl.BlockSpec(memory_space=pl.ANY)],
        out_specs=pl.BlockSpec((NA_BLOCK, HD), lambda h, b, i: (rb0 + b * nblk + i, h)),
        out_shape=jax.ShapeDtypeStruct(ob.shape, ob.dtype),
        input_output_aliases={2 * n_win + 4: 0},
        compiler_params=_params(("parallel", "parallel", "parallel")),
        name="na_attn",
    )(*([proj] * (2 * n_win + 3)), bias, ob)


def _na_meta_kernel(q_ref, km_ref, vm_ref, prev_ref, o_ref):
    del prev_ref
    q = jnp.concatenate([q_ref[...], jnp.zeros((LANES - N_META, HD), q_ref.dtype)], axis=0)
    km = _pad_meta(km_ref[...])
    vm = _pad_meta(vm_ref[...])
    lane = lax.broadcasted_iota(I32, (LANES, LANES), 1)
    s = jnp.where(lane < N_META, _qk(q, km), NEG)
    m = jnp.max(s, axis=-1, keepdims=True)
    p = jnp.exp2(s - m)
    o = jnp.dot(p.astype(vm.dtype), vm, preferred_element_type=F32) / jnp.sum(p, axis=-1, keepdims=True)
    o_ref[...] = o[:N_META].astype(o_ref.dtype)


def _na_meta_attn(proj, ob, *, w_a, w_b, n_seq, meta_row0):
    h_b = w_b // HD
    mb0 = meta_row0 // N_META
    qcol, kcol, vcol = 3 * w_a // HD, (3 * w_a + w_b) // HD, (3 * w_a + 2 * w_b) // HD
    return pl.pallas_call(
        _na_meta_kernel,
        grid=(n_seq, h_b),
        in_specs=[pl.BlockSpec((N_META, HD), lambda s, h: (mb0 + s, qcol + h)),
                  pl.BlockSpec((N_META, HD), lambda s, h: (mb0 + s, kcol + h)),
                  pl.BlockSpec((N_META, HD), lambda s, h: (mb0 + s, vcol + h)),
                  pl.BlockSpec(memory_space=pl.ANY)],
        out_specs=pl.BlockSpec((N_META, HD), lambda s, h: (mb0 + s, h)),
        out_shape=jax.ShapeDtypeStruct(ob.shape, ob.dtype),
        input_output_aliases={3: 0},
        compiler_params=_params(("parallel", "parallel")),
        name="na_meta_attn",
    )(proj, proj, proj, ob)


def _merge_kernel(oa_ref, ob_ref, wa_ref, wb_ref, ga_ref, gb_ref, o_ref):
    a = jnp.dot(oa_ref[...], wa_ref[...], preferred_element_type=F32)
    b = jnp.dot(ob_ref[...], wb_ref[...], preferred_element_type=F32)
    o_ref[...] = (ga_ref[...].astype(F32) * a + gb_ref[...].astype(F32) * b).astype(o_ref.dtype)


def _merge(oa, ob, wa, wb, proj, gate_col0):
    m, w_a = oa.shape
    w_b = ob.shape[1]
    d = wa.shape[1]
    tm = min(ROW_TILE, m)
    tn = min(COL_TILE, d)
    ga0 = gate_col0 // tn
    gb0 = (gate_col0 + d) // tn
    return pl.pallas_call(
        _merge_kernel,
        grid=(d // tn, m // tm),
        in_specs=[pl.BlockSpec((tm, w_a), lambda j, i: (i, 0)),
                  pl.BlockSpec((tm, w_b), lambda j, i: (i, 0)),
                  pl.BlockSpec((w_a, tn), lambda j, i: (0, j)),
                  pl.BlockSpec((w_b, tn), lambda j, i: (0, j)),
                  pl.BlockSpec((tm, tn), lambda j, i: (i, ga0 + j)),
                  pl.BlockSpec((tm, tn), lambda j, i: (i, gb0 + j))],
        out_specs=pl.BlockSpec((tm, tn), lambda j, i: (i, j)),
        out_shape=jax.ShapeDtypeStruct((m, d), BF16),
        compiler_params=_params(("parallel", "parallel")),
        name="merge",
    )(oa, ob, wa, wb, proj, proj)


def _outproj_kernel(x_ref, w_ref, h_ref, o_ref):
    o_ref[...] = h_ref[...] + jnp.dot(x_ref[...], w_ref[...], preferred_element_type=F32)


def _outproj(merged, w, h):
    m, d = merged.shape
    n = w.shape[1]
    tm = min(ROW_TILE, m)
    tn = min(COL_TILE, n)
    return pl.pallas_call(
        _outproj_kernel,
        grid=(n // tn, m // tm),
        in_specs=[pl.BlockSpec((tm, d), lambda j, i: (i, 0)),
                  pl.BlockSpec((d, tn), lambda j, i: (0, j)),
                  pl.BlockSpec((tm, tn), lambda j, i: (i, j))],
        out_specs=pl.BlockSpec((tm, tn), lambda j, i: (i, j)),
        out_shape=jax.ShapeDtypeStruct((m, n), F32),
        compiler_params=_params(("parallel", "parallel")),
        name="outproj",
    )(merged, w, h)


def _pack_bf16_pairs(x):
    half = x.shape[1] // 2
    lo = lax.bitcast_convert_type(x[:, :half].astype(jnp.bfloat16).astype(F32), U32)
    hi = lax.bitcast_convert_type(x[:, half:].astype(jnp.bfloat16).astype(F32), U32)
    return (lo >> 16) | (hi & jnp.uint32(0xFFFF0000))


def _unpack_bf16_pairs(u):
    lo = lax.bitcast_convert_type(u << 16, F32).astype(BF16)
    hi = lax.bitcast_convert_type(u & jnp.uint32(0xFFFF0000), F32).astype(BF16)
    return jnp.concatenate([lo, hi], axis=1)


def _router_kernel(h_ref, g_ref, wr_ref, br_ref, hf_ref, ti_ref, tw_ref):
    x = h_ref[...]
    ms = jnp.mean(x * x, axis=-1, keepdims=True)
    hf = x * lax.rsqrt(ms + EPS) * g_ref[...]
    hf_ref[...] = _pack_bf16_pairs(hf)
    logits = jnp.dot(hf, wr_ref[...], preferred_element_type=F32,
                     precision=lax.Precision.HIGHEST) + br_ref[...]
    lane = lax.broadcasted_iota(I32, logits.shape, 1).astype(F32)
    vals, idxs = [], []
    for _ in range(TOP_K):
        mx = jnp.max(logits, axis=-1, keepdims=True)
        ix = jnp.min(jnp.where(logits == mx, lane, float(LANES)), axis=-1, keepdims=True)
        vals.append(mx)
        idxs.append(ix)
        logits = jnp.where(lane == ix, -jnp.inf, logits)
    es = [jnp.exp(v - vals[0]) for v in vals]
    tot = es[0]
    for e in es[1:]:
        tot = tot + e
    ti = jnp.zeros(lane.shape, F32)
    tw = jnp.zeros(lane.shape, F32)
    for k in range(TOP_K):
        ti = jnp.where(lane == k, idxs[k], ti)
        tw = jnp.where(lane == k, es[k] / tot, tw)
    ti_ref[...] = ti.astype(I32)
    tw_ref[...] = tw


def _router(h1, g, w_router, b_router):
    m, d = h1.shape
    e = w_router.shape[1]
    assert e <= LANES
    tm = min(NORM_TILE, m)
    wr = jnp.zeros((d, LANES), F32).at[:, :e].set(w_router.astype(F32))
    br = jnp.full((1, LANES), NEG, F32).at[0, :e].set(b_router.astype(F32))
    return pl.pallas_call(
        _router_kernel,
        grid=(m // tm,),
        in_specs=[pl.BlockSpec((tm, d), lambda i: (i, 0)),
                  pl.BlockSpec((1, d), lambda i: (0, 0)),
                  pl.BlockSpec((d, LANES), lambda i: (0, 0)),
                  pl.BlockSpec((1, LANES), lambda i: (0, 0))],
        out_specs=[pl.BlockSpec((tm, d // 2), lambda i: (i, 0)),
                   pl.BlockSpec((tm, LANES), lambda i: (i, 0)),
                   pl.BlockSpec((tm, LANES), lambda i: (i, 0))],
        out_shape=[jax.ShapeDtypeStruct((m, d // 2), U32),
                   jax.ShapeDtypeStruct((m, LANES), I32),
                   jax.ShapeDtypeStruct((m, LANES), F32)],
        compiler_params=_params(("parallel",)),
        name="router",
    )(h1, g.reshape(1, d).astype(F32), wr, br)


def _row_copy(src_ref, src_row, dst_ref, dst_row, sem):
    return pltpu.make_async_copy(src_ref.at[pl.ds(src_row, 1)], dst_ref.at[pl.ds(dst_row, 1)], sem)


def _gather_kernel(nu_ref, idx_ref, idx_next_ref, src_ref, o_ref, buf, sems):
    b = pl.program_id(0)
    nu = nu_ref[0]
    rows = buf.shape[1]

    def issue(rows_ref, slot):
        def body(r, c):
            _row_copy(src_ref, rows_ref[0, r], buf.at[slot], r, sems.at[slot]).start()
            return c
        lax.fori_loop(0, rows, body, 0, unroll=DMA_UNROLL)

    def drain(slot):
        def body(r, c):
            _row_copy(src_ref, 0, buf.at[slot], r, sems.at[slot]).wait()
            return c
        lax.fori_loop(0, rows, body, 0, unroll=DMA_UNROLL)

    @pl.when(jnp.logical_and(b == 0, nu > 0))
    def _():
        issue(idx_ref, 0)

    for slot in (0, 1):
        @pl.when(jnp.logical_and(b % 2 == slot, b < nu))
        def _():
            @pl.when(b + 1 < nu)
            def _():
                issue(idx_next_ref, 1 - slot)
            drain(slot)
            o_ref[...] = _unpack_bf16_pairs(buf[slot])

    @pl.when(b >= nu)
    def _():
        o_ref[...] = jnp.zeros(o_ref.shape, o_ref.dtype)


def _gather_rows(src, slot_rows, n_used, tile):
    n_slots = slot_rows.shape[0]
    half = src.shape[1]
    nb = n_slots // tile
    slot_blocks = slot_rows.reshape(nb, 1, tile)
    return pl.pallas_call(
        _gather_kernel,
        grid=(nb,),
        in_specs=[pl.BlockSpec(memory_space=pltpu.SMEM),
                  pl.BlockSpec((None, 1, tile), lambda b: (b, 0, 0), memory_space=pltpu.SMEM),
                  pl.BlockSpec((None, 1, tile), lambda b: (jnp.minimum(b + 1, nb - 1), 0, 0),
                               memory_space=pltpu.SMEM),
                  pl.BlockSpec(memory_space=pl.ANY)],
        out_specs=pl.BlockSpec((tile, 2 * half), lambda b: (b, 0)),
        out_shape=jax.ShapeDtypeStruct((n_slots, 2 * half), BF16),
        scratch_shapes=[pltpu.VMEM((2, tile, half), src.dtype), pltpu.SemaphoreType.DMA((2,))],
        compiler_params=_params(("arbitrary",)),
        name="moe_gather",
    )(n_used, slot_blocks, slot_blocks, src)


def _swiglu(x, wg, wu, bg, bu):
    g = jnp.dot(x, wg, preferred_element_type=F32) + bg
    u = jnp.dot(x, wu, preferred_element_type=F32) + bu
    g = jnp.minimum(g, SWIGLU_LIMIT)
    u = jnp.clip(u, -SWIGLU_LIMIT, SWIGLU_LIMIT)
    return ((u + 1.0) * (g * (1.0 / (1.0 + jnp.exp(-SWIGLU_ALPHA * g))))).astype(BF16)


def _moe_up_kernel(be_ref, nu_ref, nv_ref, x_ref, wg_ref, wu_ref, bg_ref, bu_ref, o_ref):
    del be_ref
    b = pl.program_id(0)
    half = x_ref.shape[0] // 2

    @pl.when(b < nu_ref[0])
    def _():
        wg = wg_ref[0].astype(BF16)
        wu = wu_ref[0].astype(BF16)

        @pl.when(nv_ref[b] > half)
        def _():
            o_ref[...] = _swiglu(x_ref[...], wg, wu, bg_ref[0], bu_ref[0])

        @pl.when(nv_ref[b] <= half)
        def _():
            o_ref[:half] = _swiglu(x_ref[:half], wg, wu, bg_ref[0], bu_ref[0])
            o_ref[half:] = jnp.zeros((half, o_ref.shape[1]), o_ref.dtype)

    @pl.when(b >= nu_ref[0])
    def _():
        o_ref[...] = jnp.zeros(o_ref.shape, o_ref.dtype)


def _moe_down_kernel(be_ref, nu_ref, nv_ref, a_ref, wd_ref, bd_ref, o_ref):
    del be_ref
    b = pl.program_id(0)
    half = a_ref.shape[0] // 2

    @pl.when(b < nu_ref[0])
    def _():
        wd = wd_ref[0].astype(BF16)

        @pl.when(nv_ref[b] > half)
        def _():
            o_ref[...] = jnp.dot(a_ref[...], wd, preferred_element_type=F32) + bd_ref[0]

        @pl.when(nv_ref[b] <= half)
        def _():
            o_ref[:half] = jnp.dot(a_ref[:half], wd, preferred_element_type=F32) + bd_ref[0]
            o_ref[half:] = jnp.zeros((half, o_ref.shape[1]), o_ref.dtype)

    @pl.when(b >= nu_ref[0])
    def _():
        o_ref[...] = jnp.zeros(o_ref.shape, o_ref.dtype)


def _moe_experts(xs, block_e, n_used, n_valid, wg, wu, wd, bg, bu, bd, tile):
    n_slots, d = xs.shape
    e, _, ff = wg.shape
    tf = min(MOE_FF_TILE, ff)
    tn = min(MOE_DOWN_TILE, d)
    nb, nf, nn = n_slots // tile, ff // tf, d // tn

    def bb(b, nu):
        return jnp.minimum(b, nu[0] - 1)

    def frozen(b, j, last, nu):
        return jnp.where(b < nu[0], j, last)

    up_spec = pltpu.PrefetchScalarGridSpec(
        num_scalar_prefetch=3,
        grid=(nb, nf),
        in_specs=[
            pl.BlockSpec((tile, d), lambda b, f, be, nu, nv: (bb(b, nu), 0)),
            pl.BlockSpec((1, d, tf), lambda b, f, be, nu, nv: (be[bb(b, nu)], 0, frozen(b, f, nf - 1, nu))),
            pl.BlockSpec((1, d, tf), lambda b, f, be, nu, nv: (be[bb(b, nu)], 0, frozen(b, f, nf - 1, nu))),
            pl.BlockSpec((1, 1, tf), lambda b, f, be, nu, nv: (be[bb(b, nu)], 0, frozen(b, f, nf - 1, nu))),
            pl.BlockSpec((1, 1, tf), lambda b, f, be, nu, nv: (be[bb(b, nu)], 0, frozen(b, f, nf - 1, nu))),
        ],
        out_specs=pl.BlockSpec((tile, tf), lambda b, f, be, nu, nv: (b, f)),
    )
    act = pl.pallas_call(
        _moe_up_kernel,
        grid_spec=up_spec,
        out_shape=jax.ShapeDtypeStruct((n_slots, ff), BF16),
        compiler_params=_params(("arbitrary", "arbitrary")),
        name="moe_up",
    )(block_e, n_used, n_valid, xs, wg, wu, bg.reshape(e, 1, ff), bu.reshape(e, 1, ff))

    down_spec = pltpu.PrefetchScalarGridSpec(
        num_scalar_prefetch=3,
        grid=(nb, nn),
        in_specs=[
            pl.BlockSpec((tile, ff), lambda b, n, be, nu, nv: (bb(b, nu), 0)),
            pl.BlockSpec((1, ff, tn), lambda b, n, be, nu, nv: (be[bb(b, nu)], 0, frozen(b, n, nn - 1, nu))),
            pl.BlockSpec((1, 1, tn), lambda b, n, be, nu, nv: (be[bb(b, nu)], 0, frozen(b, n, nn - 1, nu))),
        ],
        out_specs=pl.BlockSpec((tile, tn), lambda b, n, be, nu, nv: (b, n)),
    )
    return pl.pallas_call(
        _moe_down_kernel,
        grid_spec=down_spec,
        out_shape=jax.ShapeDtypeStruct((n_slots, d), F32),
        compiler_params=_params(("arbitrary", "arbitrary")),
        name="moe_down",
    )(block_e, n_used, n_valid, act, wd, bd.reshape(e, 1, d))


def _combine_kernel(pos_ref, pos_next_ref, w_ref, h_ref, ys_ref, o_ref, buf, sems, *, n):
    i = pl.program_id(0)
    rows = o_ref.shape[0]

    def issue(rows_ref, slot):
        def body(r, c):
            for k in range(TOP_K):
                _row_copy(ys_ref, rows_ref[0, TOP_K * r + k], buf.at[slot, k], r, sems.at[slot]).start()
            return c
        lax.fori_loop(0, rows, body, 0, unroll=DMA_UNROLL // 2)

    def drain(slot):
        def body(r, c):
            for k in range(TOP_K):
                _row_copy(ys_ref, 0, buf.at[slot, k], r, sems.at[slot]).wait()
            return c
        lax.fori_loop(0, rows, body, 0, unroll=DMA_UNROLL // 2)

    @pl.when(i == 0)
    def _():
        issue(pos_ref, 0)

    for slot in (0, 1):
        @pl.when(i % 2 == slot)
        def _():
            @pl.when(i + 1 < n)
            def _():
                issue(pos_next_ref, 1 - slot)
            drain(slot)
            w = w_ref[...]
            acc = h_ref[...]
            for k in range(TOP_K):
                acc = acc + w[:, k:k + 1] * buf[slot, k]
            o_ref[...] = acc


def _combine(h1, tw, pos, ys, row0, n_rows):
    d = h1.shape[1]
    tc = min(COMBINE_TILE, n_rows)
    assert n_rows % tc == 0 and row0 % tc == 0
    nb, b0 = n_rows // tc, row0 // tc
    pos_blocks = pos[row0:row0 + n_rows].reshape(nb, 1, tc * TOP_K)
    return pl.pallas_call(
        functools.partial(_combine_kernel, n=nb),
        grid=(nb,),
        in_specs=[pl.BlockSpec((None, 1, tc * TOP_K), lambda i: (i, 0, 0), memory_space=pltpu.SMEM),
                  pl.BlockSpec((None, 1, tc * TOP_K), lambda i: (jnp.minimum(i + 1, nb - 1), 0, 0),
                               memory_space=pltpu.SMEM),
                  pl.BlockSpec((tc, LANES), lambda i: (b0 + i, 0)),
                  pl.BlockSpec((tc, d), lambda i: (b0 + i, 0)),
                  pl.BlockSpec(memory_space=pl.ANY)],
        out_specs=pl.BlockSpec((tc, d), lambda i: (i, 0)),
        out_shape=jax.ShapeDtypeStruct((n_rows, d), F32),
        scratch_shapes=[pltpu.VMEM((2, TOP_K, tc, d), F32), pltpu.SemaphoreType.DMA((2,))],
        compiler_params=_params(("arbitrary",)),
        name="moe_combine",
    )(pos_blocks, pos_blocks, tw, h1, ys)


def _dispatch_plan(top_i, n_experts, tile):
    n_tok = top_i.shape[0]
    nk = n_tok * TOP_K
    flat_e = top_i.reshape(-1)
    onehot = (flat_e[:, None] == jnp.arange(n_experts, dtype=I32)[None, :]).astype(I32)
    rank = jnp.take_along_axis(jnp.cumsum(onehot, axis=0), flat_e[:, None], axis=1)[:, 0] - 1
    counts = jnp.sum(onehot, axis=0)
    nblk = (counts + tile - 1) // tile
    blk_end = jnp.cumsum(nblk)
    blk_start = blk_end - nblk
    pos = blk_start[flat_e] * tile + rank
    nb_max = -(-nk // tile) + n_experts
    block_e = jnp.clip(jnp.searchsorted(blk_end, jnp.arange(nb_max, dtype=I32), side="right"),
                       0, n_experts - 1).astype(I32)
    n_used = blk_end[-1:].astype(I32)
    blocks = jnp.arange(nb_max, dtype=I32)
    n_valid = jnp.clip(counts[block_e] - (blocks - blk_start[block_e]) * tile, 0, tile)
    n_valid = jnp.where(blocks < n_used[0], n_valid, 0).astype(I32)
    return pos.astype(I32), block_e, n_used, n_valid, nb_max


def _round_up(x, m):
    return -(-x // m) * m


def kernel(x_prompt, x_sample, meta_tokens, norm_mix, w_in, qk_norm_a_q, qk_norm_a_k, lambda_q1, lambda_k1, lambda_q2, lambda_k2, subln_a, rel_bias, qk_norm_b_q, qk_norm_b_k, na_rpb, w_br_a, w_br_b, w_out, norm_ffn, w_router, b_router, w_gate, b_gate, w_up, b_up, w_down, b_down):
    d = x_prompt.shape[-1]
    w_a = w_br_a.shape[1]
    w_b = w_br_b.shape[1]
    n_experts = w_router.shape[-1]
    groups = [x_prompt, x_sample]

    row0s, seqs = [], []
    m_real = 0
    for x in groups:
        row0s.append(m_real)
        m_real += x.shape[0] * x.shape[1]
    n_seq = sum(x.shape[0] for x in groups)
    meta_row0 = m_real
    m_tok = m_real + n_seq * N_META
    m_pad = _round_up(m_tok + 1, ROW_TILE)
    h_all = jnp.concatenate(
        [x.reshape(-1, d) for x in groups]
        + [jnp.tile(meta_tokens.astype(F32), (n_seq, 1)), jnp.zeros((m_pad - m_tok, d), F32)], axis=0)

    hn = _rmsnorm_rows(h_all, norm_mix[0], BF16)
    scale_a, scale_b = HD ** -0.5 * LOG2E, HD ** -0.5 * LOG2E
    ones = lambda n: jnp.ones((n,), F32)
    gain = jnp.concatenate([
        jnp.tile(qk_norm_a_q[0].astype(F32) * scale_a, w_a // HD), jnp.tile(qk_norm_a_k[0].astype(F32), w_a // HD),
        ones(w_a),
        jnp.tile(qk_norm_b_q[0].astype(F32) * scale_b, w_b // HD), jnp.tile(qk_norm_b_k[0].astype(F32), w_b // HD),
        ones(w_b + 2 * d)]).reshape(1, -1)
    proj = _inproj(hn, w_in[0].astype(BF16), gain, w_a, w_b)

    lam_vecs = [v[0].reshape(1, HD).astype(F32) for v in (lambda_q1, lambda_k1, lambda_q2, lambda_k2)]
    subln = subln_a[0].reshape(1, 2 * HD).astype(F32)
    na_bias = _na_bias_tiles(na_rpb[0].astype(F32) * LOG2E)
    rel_bias2 = rel_bias.astype(F32) * LOG2E
    oa = jnp.zeros((m_pad, w_a), BF16)
    ob = jnp.zeros((m_pad, w_b), BF16)
    tile_cache = {}
    seq0 = 0
    for x, row0 in zip(groups, row0s):
        nb, s = x.shape[0], x.shape[1]
        mrow = meta_row0 + seq0 * N_META
        tq = min(ATTN_TQ, s)
        common = dict(w_a=w_a, n_seq=nb, seq_len=s, kv_row0=row0, meta_row0=mrow, tk=min(ATTN_TK, s))
        oa = _diff_attn(proj, oa, rel_bias2, lam_vecs, subln, tile_cache, q_row0=row0,
                        q_stride=s, tq=tq, nq=s // tq, qpos0=N_META, **common)
        oa = _diff_attn(proj, oa, rel_bias2, lam_vecs, subln, tile_cache, q_row0=mrow,
                        q_stride=N_META, tq=N_META, nq=1, qpos0=0, **common)
        ob = _na_attn(proj, ob, na_bias, w_a=w_a, w_b=w_b, n_seq=nb, seq_len=s, row0=row0, meta_row0=mrow)
        seq0 += nb
    ob = _na_meta_attn(proj, ob, w_a=w_a, w_b=w_b, n_seq=n_seq, meta_row0=meta_row0)

    merged = _merge(oa, ob, w_br_a[0].astype(BF16), w_br_b[0].astype(BF16), proj, 3 * w_a + 3 * w_b)
    h1 = _outproj(merged, w_out[0].astype(BF16), h_all)

    hf, top_i, top_w = _router(h1, norm_ffn[0], w_router[0], b_router[0])
    pos, block_e, n_used, n_valid, nb_max = _dispatch_plan(top_i[:m_tok, :TOP_K], n_experts, MOE_TILE)
    slot_rows = jnp.full((nb_max * MOE_TILE,), m_tok, I32).at[pos].set(
        jnp.repeat(jnp.arange(m_tok, dtype=I32), TOP_K))
    xs = _gather_rows(hf, slot_rows, n_used, MOE_TILE)
    ys = _moe_experts(xs, block_e, n_used, n_valid, w_gate[0].astype(F32), w_up[0].astype(F32),
                      w_down[0].astype(F32), b_gate[0].astype(F32), b_up[0].astype(F32),
                      b_down[0].astype(F32), MOE_TILE)
    pos2 = pos.reshape(m_tok, TOP_K)
    outs = []
    for x, row0 in zip(groups, row0s):
        n_rows = x.shape[0] * x.shape[1]
        outs.append(_combine(h1, top_w, pos2, ys, row0, n_rows).reshape(x.shape))
    return tuple(outs)
```

```python
import functools
import math

import numpy as np
import jax
import jax.numpy as jnp
from jax import lax
from jax.experimental import pallas as pl
from jax.experimental.pallas import tpu as pltpu

F32 = jnp.float32
BF16 = jnp.bfloat16
U32 = jnp.uint32
I32 = jnp.int32

N_META = 16
GRID_W = 64
HD = 128
REL_BUCKETS = 32
REL_MAX_DIST = 128
NA_MAX_ROWS = 8
NA_COLS = 16
TOP_K = 4
SWIGLU_LIMIT = 7.0
SWIGLU_ALPHA = 1.702
EPS = 1e-6
LAMBDA_INIT = 0.8 - 0.6 * math.exp(-0.3 * 0)
NEG = -1e30
LOG2E = math.log2(math.e)

LANES = 128
VMEM_LIMIT_MB = 56
ROW_TILE = 1024
COL_TILE = 1024
NORM_TILE = 256
NA_ROWS = 8
NA_BLOCK = NA_ROWS * GRID_W
NA_WIN_ROWS = NA_ROWS + NA_MAX_ROWS
NA_WIN = NA_WIN_ROWS * GRID_W
NA_KBLOCK = NA_WIN // 4
MOE_TILE = 1536
MOE_SUB_TILE = 512
MOE_FF_TILE = 256
MOE_DOWN_TILE = 256
ATTN_TQ = 1024
ATTN_TK = 2048
COMBINE_TILE = 128
DMA_UNROLL = 8


def _params(semantics, vmem_mb=VMEM_LIMIT_MB):
    return pltpu.CompilerParams(dimension_semantics=semantics,
                                vmem_limit_bytes=vmem_mb * 2**20)


def _rmsnorm_kernel(x_ref, g_ref, o_ref):
    x = x_ref[...].astype(F32)
    ms = jnp.mean(x * x, axis=-1, keepdims=True)
    o_ref[...] = (x * lax.rsqrt(ms + EPS) * g_ref[...]).astype(o_ref.dtype)


def _rmsnorm_rows(x, g, out_dtype):
    m, d = x.shape
    tm = min(NORM_TILE, m)
    return pl.pallas_call(
        _rmsnorm_kernel,
        grid=(m // tm,),
        in_specs=[pl.BlockSpec((tm, d), lambda i: (i, 0)),
                  pl.BlockSpec((1, d), lambda i: (0, 0))],
        out_specs=pl.BlockSpec((tm, d), lambda i: (i, 0)),
        out_shape=jax.ShapeDtypeStruct((m, d), out_dtype),
        compiler_params=_params(("parallel",)),
        name="rmsnorm_rows",
    )(x, g.reshape(1, d).astype(F32))


def _inproj_kernel(x_ref, w_ref, gain_ref, o_ref, *, norm_ranges, gate_start):
    j = pl.program_id(0)
    acc = jnp.dot(x_ref[...], w_ref[...], preferred_element_type=F32)
    tn = acc.shape[1]

    is_norm = False
    for lo, hi in norm_ranges:
        is_norm = jnp.logical_or(is_norm, jnp.logical_and(j >= lo, j < hi))
    is_gate = j >= gate_start

    @pl.when(is_norm)
    def _():
        for c in range(tn // HD):
            blk = acc[:, c * HD:(c + 1) * HD]
            ms = jnp.mean(blk * blk, axis=-1, keepdims=True)
            o_ref[:, c * HD:(c + 1) * HD] = (
                blk * lax.rsqrt(ms + EPS) * gain_ref[:, c * HD:(c + 1) * HD]).astype(o_ref.dtype)

    @pl.when(is_gate)
    def _():
        o_ref[...] = (1.0 / (1.0 + jnp.exp(-acc))).astype(o_ref.dtype)

    @pl.when(jnp.logical_not(jnp.logical_or(is_norm, is_gate)))
    def _():
        o_ref[...] = acc.astype(o_ref.dtype)


def _inproj(hn, w, gain, w_a, w_b):
    m, d = hn.shape
    n = w.shape[1]
    tm = min(ROW_TILE, m)
    tn = min(COL_TILE, w_a, w_b)
    assert w_a % tn == 0 and w_b % tn == 0 and n % tn == 0 and m % tm == 0
    norm_ranges = ((0, 2 * w_a // tn), (3 * w_a // tn, (3 * w_a + 2 * w_b) // tn))
    gate_start = (3 * w_a + 3 * w_b) // tn
    kern = functools.partial(_inproj_kernel, norm_ranges=norm_ranges, gate_start=gate_start)
    return pl.pallas_call(
        kern,
        grid=(n // tn, m // tm),
        in_specs=[pl.BlockSpec((tm, d), lambda j, i: (i, 0)),
                  pl.BlockSpec((d, tn), lambda j, i: (0, j)),
                  pl.BlockSpec((1, tn), lambda j, i: (0, j))],
        out_specs=pl.BlockSpec((tm, tn), lambda j, i: (i, j)),
        out_shape=jax.ShapeDtypeStruct((m, n), BF16),
        compiler_params=_params(("parallel", "parallel")),
        name="inproj",
    )(hn, w, gain)


def _t5_bucket(rel):
    half = REL_BUCKETS // 2
    max_exact = half // 2
    sign = (rel > 0).astype(I32) * half
    n = jnp.abs(rel)
    nf = jnp.maximum(n, 1).astype(F32)
    large = max_exact + (jnp.log(nf / max_exact) / math.log(REL_MAX_DIST / max_exact)
                         * (half - max_exact)).astype(I32)
    large = jnp.minimum(large, half - 1)
    return sign + jnp.where(n < max_exact, n, large)


def _bias_tile_plan(qpos0, nq, tq, nk, tk):
    keys, ids = [], np.zeros((nq, nk), np.int32)
    for i in range(nq):
        for j in range(nk):
            d = (N_META + j * tk) - (qpos0 + i * tq)
            if d - (tq - 1) >= REL_MAX_DIST:
                key = ("far", REL_MAX_DIST)
            elif d + (tk - 1) <= -REL_MAX_DIST:
                key = ("far", -REL_MAX_DIST)
            else:
                key = ("near", d)
            if key not in keys:
                keys.append(key)
            ids[i, j] = keys.index(key)
    return tuple(keys), ids.reshape(-1)


def _bias_tiles(rel_bias, keys, tq, tk):
    h = rel_bias.shape[1]
    blocked = tq % LANES == 0 and tk % LANES == 0 and all(v % LANES == 0 for _, v in keys)
    if not blocked:
        ramp = np.arange(tk)[None, :] - np.arange(tq)[:, None]
        rel = np.stack([np.full((tq, tk), v) if kind == "far" else v + ramp for kind, v in keys])
        b = rel_bias[_t5_bucket(jnp.asarray(rel.astype(np.int32)))]
        return jnp.transpose(b, (3, 0, 1, 2)).astype(F32)

    def const(v):
        return rel_bias[_t5_bucket(jnp.full((1,), v, I32))].reshape(h, 1, 1, 1, 1)

    r = np.arange(LANES)
    rel3 = np.stack([LANES * dl + r[None, :] - r[:, None] for dl in (-1, 0, 1)]).astype(np.int32)
    f3 = jnp.transpose(rel_bias[_t5_bucket(jnp.asarray(rel3))], (3, 0, 1, 2))
    na, nc = tq // LANES, tk // LANES
    tiles = []
    for kind, v in keys:
        if kind == "far":
            tiles.append(jnp.broadcast_to(const(v).reshape(h, 1, 1), (h, tq, tk)))
            continue
        delta = v // LANES + np.arange(nc)[None, :] - np.arange(na)[:, None]
        t5 = jnp.where(jnp.asarray(delta < 0)[None, :, None, :, None], const(-REL_MAX_DIST), const(REL_MAX_DIST))
        for dl in (-1, 0, 1):
            t5 = jnp.where(jnp.asarray(delta == dl)[None, :, None, :, None], f3[:, dl + 1][:, None, :, None, :], t5)
        tiles.append(t5.reshape(h, tq, tk))
    return jnp.stack(tiles, axis=1).astype(F32)


def _meta_bias_tiles(rel_bias, qpos0, nq, tq):
    n = 1
    while n < nq and qpos0 + (n - 1) * tq - (N_META - 1) < REL_MAX_DIST:
        n += 1
    qpos = qpos0 + np.arange(n * tq).reshape(n, tq)
    rel = np.arange(N_META)[None, None, :] - qpos[:, :, None]
    b = jnp.transpose(rel_bias[_t5_bucket(jnp.asarray(rel.astype(np.int32)))], (3, 0, 1, 2)).astype(F32)
    pad = jnp.full(b.shape[:3] + (LANES - N_META,), NEG, F32)
    return jnp.concatenate([b, pad], axis=-1), n


def _softmax_step(s, m_ref, l_ref, a_ref, v, first):
    m_cur = jnp.max(s, axis=-1, keepdims=True)
    if first:
        m_new = m_cur
        p = jnp.exp2(s - m_new)
        l_ref[...] = jnp.sum(p, axis=-1, keepdims=True)
        a_ref[...] = jnp.dot(p.astype(v.dtype), v, preferred_element_type=F32)
    else:
        m_prev = m_ref[...]
        m_new = jnp.maximum(m_prev, m_cur)
        alpha = jnp.exp2(m_prev - m_new)
        p = jnp.exp2(s - m_new)
        l_ref[...] = alpha * l_ref[...] + jnp.sum(p, axis=-1, keepdims=True)
        a_ref[...] = alpha * a_ref[...] + jnp.dot(p.astype(v.dtype), v, preferred_element_type=F32)
    m_ref[...] = m_new


def _qk(q, k):
    return lax.dot_general(q, k, (((1,), (1,)), ((), ())), preferred_element_type=F32)


def _pad_meta(x):
    return jnp.concatenate([x, jnp.zeros((LANES - N_META, x.shape[1]), x.dtype)], axis=0)


def _diff_attn_kernel(tid_ref, q_ref, k_ref, v_ref, km_ref, vm_ref, bias_ref, biasm_ref,
                      lq1_ref, lk1_ref, lq2_ref, lk2_ref, sub_ref, prev_ref, o_ref,
                      m1, l1, a1, m2, l2, a2, *, nk):
    del tid_ref, prev_ref
    j = pl.program_id(3)
    q = q_ref[...]
    q1, q2 = q[:, :HD], q[:, HD:]

    @pl.when(j == 0)
    def _():
        km = _pad_meta(km_ref[...])
        vm = _pad_meta(vm_ref[...])
        bm = biasm_ref[0, 0]
        _softmax_step(_qk(q1, km[:, :HD]) + bm, m1, l1, a1, vm, True)
        _softmax_step(_qk(q2, km[:, HD:]) + bm, m2, l2, a2, vm, True)

    k = k_ref[...]
    v = v_ref[...]
    b = bias_ref[0, 0]
    _softmax_step(_qk(q1, k[:, :HD]) + b, m1, l1, a1, v, False)
    _softmax_step(_qk(q2, k[:, HD:]) + b, m2, l2, a2, v, False)

    @pl.when(j == nk - 1)
    def _():
        lam = (jnp.exp(jnp.sum(lq1_ref[...] * lk1_ref[...], axis=-1, keepdims=True))
               - jnp.exp(jnp.sum(lq2_ref[...] * lk2_ref[...], axis=-1, keepdims=True))
               + LAMBDA_INIT)
        o = a1[...] / l1[...] - lam * (a2[...] / l2[...])
        ms = jnp.mean(o * o, axis=-1, keepdims=True)
        o_ref[...] = (o * lax.rsqrt(ms + EPS) * sub_ref[...] * (1.0 - LAMBDA_INIT)).astype(o_ref.dtype)


def _diff_attn(proj, oa, rel_bias, lam_vecs, subln, tile_cache, *, w_a, n_seq, seq_len, kv_row0,
               q_row0, q_stride, tq, nq, qpos0, meta_row0, tk):
    h_a = w_a // (2 * HD)
    nk = seq_len // tk
    assert seq_len % tk == 0 and kv_row0 % tk == 0 and q_row0 % tq == 0 and q_stride % tq == 0
    assert meta_row0 % N_META == 0
    keys, ids = _bias_tile_plan(qpos0, nq, tq, nk, tk)
    if (keys, tq, tk) not in tile_cache:
        tile_cache[(keys, tq, tk)] = _bias_tiles(rel_bias, keys, tq, tk)
    tiles = tile_cache[(keys, tq, tk)]
    biasm, n_bm = _meta_bias_tiles(rel_bias, qpos0, nq, tq)
    qb0, qbs = q_row0 // tq, q_stride // tq
    kb0, kbs = kv_row0 // tk, seq_len // tk
    mb0 = meta_row0 // N_META
    kcol, vcol = w_a // (2 * HD), 2 * w_a // (2 * HD)
    dv = 2 * HD

    def qmap(b, h, i, j, t):
        return (qb0 + b * qbs + i, h)

    vec = pl.BlockSpec((1, HD), lambda b, h, i, j, t: (0, 0))
    grid_spec = pltpu.PrefetchScalarGridSpec(
        num_scalar_prefetch=1,
        grid=(n_seq, h_a, nq, nk),
        in_specs=[
            pl.BlockSpec((tq, dv), qmap),
            pl.BlockSpec((tk, dv), lambda b, h, i, j, t: (kb0 + b * kbs + j, kcol + h)),
            pl.BlockSpec((tk, dv), lambda b, h, i, j, t: (kb0 + b * kbs + j, vcol + h)),
            pl.BlockSpec((N_META, dv), lambda b, h, i, j, t: (mb0 + b, kcol + h)),
            pl.BlockSpec((N_META, dv), lambda b, h, i, j, t: (mb0 + b, vcol + h)),
            pl.BlockSpec((1, 1, tq, tk), lambda b, h, i, j, t: (h, t[i * nk + j], 0, 0)),
            pl.BlockSpec((1, 1, tq, LANES), lambda b, h, i, j, t: (h, jnp.minimum(i, n_bm - 1), 0, 0)),
            vec, vec, vec, vec,
            pl.BlockSpec((1, dv), lambda b, h, i, j, t: (0, 0)),
            pl.BlockSpec(memory_space=pl.ANY),
        ],
        out_specs=pl.BlockSpec((tq, dv), qmap),
        scratch_shapes=[pltpu.VMEM((tq, 1), F32), pltpu.VMEM((tq, 1), F32), pltpu.VMEM((tq, dv), F32),
                        pltpu.VMEM((tq, 1), F32), pltpu.VMEM((tq, 1), F32), pltpu.VMEM((tq, dv), F32)],
    )
    return pl.pallas_call(
        functools.partial(_diff_attn_kernel, nk=nk),
        grid_spec=grid_spec,
        out_shape=jax.ShapeDtypeStruct(oa.shape, oa.dtype),
        input_output_aliases={13: 0},
        compiler_params=_params(("parallel", "parallel", "parallel", "arbitrary")),
        name=f"diff_attn_q{tq}",
    )(jnp.asarray(ids), proj, proj, proj, proj, proj, tiles, biasm, *lam_vecs, subln, oa)


def _na_plan():
    rows_c = 3 * NA_ROWS
    a = np.arange(NA_ROWS)
    kl = np.arange(NA_WIN_ROWS)
    ridx = np.zeros((3, NA_ROWS, NA_WIN_ROWS), np.int32)
    rvalid = np.zeros((3, NA_ROWS, NA_WIN_ROWS), bool)
    for v in range(3):
        r = v * NA_ROWS + a
        rs = np.clip(r - NA_MAX_ROWS // 2, 0, rows_c - NA_MAX_ROWS)
        kr = v * NA_ROWS - NA_MAX_ROWS // 2 + kl
        off = kr[None, :] - r[:, None]
        ok = (kr[None, :] >= rs[:, None]) & (kr[None, :] < rs[:, None] + NA_MAX_ROWS)
        ok &= (kr[None, :] >= 0) & (kr[None, :] < rows_c)
        ridx[v] = np.clip(off + NA_MAX_ROWS - 1, 0, 2 * NA_MAX_ROWS - 2)
        rvalid[v] = ok
    c = np.arange(GRID_W)
    cs = np.clip(c - NA_COLS // 2, 0, GRID_W - NA_COLS)
    coff = c[None, :] - c[:, None]
    cvalid = (c[None, :] >= cs[:, None]) & (c[None, :] < cs[:, None] + NA_COLS)
    cidx = np.clip(coff + NA_COLS - 1, 0, 2 * NA_COLS - 2).astype(np.int32)
    return ridx, rvalid, cidx, cvalid


def _na_bias_tiles(rpb):
    ridx, rvalid, cidx, cvalid = _na_plan()
    h = rpb.shape[0]
    t1 = rpb[:, :, cidx]
    t2 = t1[:, ridx]
    valid = rvalid[:, :, :, None, None] & cvalid[None, None, None]
    t2 = jnp.where(jnp.asarray(valid)[None], t2, NEG)
    t2 = jnp.transpose(t2, (0, 1, 2, 4, 3, 5))
    return t2.reshape(h, 3, NA_BLOCK, NA_WIN).astype(F32)


def _na_kernel(q_ref, k0_ref, k1_ref, k2_ref, k3_ref, v0_ref, v1_ref, v2_ref, v3_ref, km_ref, vm_ref,
               bias_ref, prev_ref, o_ref):
    del prev_ref
    q = q_ref[...]
    k = jnp.concatenate([k0_ref[...], k1_ref[...], k2_ref[...], k3_ref[...]], axis=0)
    v = jnp.concatenate([v0_ref[...], v1_ref[...], v2_ref[...], v3_ref[...]], axis=0)
    s = _qk(q, k) + bias_ref[0, 0]
    km = _pad_meta(km_ref[...])
    vm = _pad_meta(vm_ref[...])
    lane = lax.broadcasted_iota(I32, (q.shape[0], LANES), 1)
    sm = jnp.where(lane < N_META, _qk(q, km), NEG)
    m = jnp.maximum(jnp.max(s, axis=-1, keepdims=True), jnp.max(sm, axis=-1, keepdims=True))
    p = jnp.exp2(s - m)
    pm = jnp.exp2(sm - m)
    l = jnp.sum(p, axis=-1, keepdims=True) + jnp.sum(pm, axis=-1, keepdims=True)
    o = (jnp.dot(p.astype(v.dtype), v, preferred_element_type=F32)
         + jnp.dot(pm.astype(vm.dtype), vm, preferred_element_type=F32))
    o_ref[...] = (o / l).astype(o_ref.dtype)


def _na_attn(proj, ob, bias, *, w_a, w_b, n_seq, seq_len, row0, meta_row0):
    h_b = w_b // HD
    nblk = seq_len // NA_BLOCK
    assert seq_len % NA_BLOCK == 0 and nblk >= 2 and row0 % NA_BLOCK == 0
    rb0 = row0 // NA_BLOCK
    mb0 = meta_row0 // N_META
    qcol, kcol, vcol = 3 * w_a // HD, (3 * w_a + w_b) // HD, (3 * w_a + 2 * w_b) // HD

    per_q = NA_BLOCK // NA_KBLOCK
    n_kb = nblk * per_q

    def kblk(col, t):
        def index(h, b, i):
            return (rb0 * per_q + b * n_kb + jnp.clip(i * per_q - 1 + t, 0, n_kb - 1), col + h)
        return pl.BlockSpec((NA_KBLOCK, HD), index)

    def variant(i):
        return jnp.where(i == 0, 0, jnp.where(i == nblk - 1, 2, 1))

    n_win = NA_WIN // NA_KBLOCK
    return pl.pallas_call(
        _na_kernel,
        grid=(h_b, n_seq, nblk),
        in_specs=[pl.BlockSpec((NA_BLOCK, HD), lambda h, b, i: (rb0 + b * nblk + i, qcol + h))]
                 + [kblk(kcol, t) for t in range(n_win)] + [kblk(vcol, t) for t in range(n_win)]
                 + [pl.BlockSpec((N_META, HD), lambda h, b, i: (mb0 + b, kcol + h)),
                    pl.BlockSpec((N_META, HD), lambda h, b, i: (mb0 + b, vcol + h)),
                    pl.BlockSpec((1, 1, NA_BLOCK, NA_WIN), lambda h, b, i: (h, variant(i), 0, 0)),
                    pl.BlockSpec(memory_space=pl.ANY)],
        out_specs=pl.BlockSpec((NA_BLOCK, HD), lambda h, b, i: (rb0 + b * nblk + i, h)),
        out_shape=jax.ShapeDtypeStruct(ob.shape, ob.dtype),
        input_output_aliases={2 * n_win + 4: 0},
        compiler_params=_params(("parallel", "parallel", "parallel")),
        name="na_attn",
    )(*([proj] * (2 * n_win + 3)), bias, ob)


def _na_meta_kernel(q_ref, km_ref, vm_ref, prev_ref, o_ref):
    del prev_ref
    q = jnp.concatenate([q_ref[...], jnp.zeros((LANES - N_META, HD), q_ref.dtype)], axis=0)
    km = _pad_meta(km_ref[...])
    vm = _pad_meta(vm_ref[...])
    lane = lax.broadcasted_iota(I32, (LANES, LANES), 1)
    s = jnp.where(lane < N_META, _qk(q, km), NEG)
    m = jnp.max(s, axis=-1, keepdims=True)
    p = jnp.exp2(s - m)
    o = jnp.dot(p.astype(vm.dtype), vm, preferred_element_type=F32) / jnp.sum(p, axis=-1, keepdims=True)
    o_ref[...] = o[:N_META].astype(o_ref.dtype)


def _na_meta_attn(proj, ob, *, w_a, w_b, n_seq, meta_row0):
    h_b = w_b // HD
    mb0 = meta_row0 // N_META
    qcol, kcol, vcol = 3 * w_a // HD, (3 * w_a + w_b) // HD, (3 * w_a + 2 * w_b) // HD
    return pl.pallas_call(
        _na_meta_kernel,
        grid=(n_seq, h_b),
        in_specs=[pl.BlockSpec((N_META, HD), lambda s, h: (mb0 + s, qcol + h)),
                  pl.BlockSpec((N_META, HD), lambda s, h: (mb0 + s, kcol + h)),
                  pl.BlockSpec((N_META, HD), lambda s, h: (mb0 + s, vcol + h)),
                  pl.BlockSpec(memory_space=pl.ANY)],
        out_specs=pl.BlockSpec((N_META, HD), lambda s, h: (mb0 + s, h)),
        out_shape=jax.ShapeDtypeStruct(ob.shape, ob.dtype),
        input_output_aliases={3: 0},
        compiler_params=_params(("parallel", "parallel")),
        name="na_meta_attn",
    )(proj, proj, proj, ob)


def _merge_kernel(oa_ref, ob_ref, wa_ref, wb_ref, ga_ref, gb_ref, o_ref):
    a = jnp.dot(oa_ref[...], wa_ref[...], preferred_element_type=F32)
    b = jnp.dot(ob_ref[...], wb_ref[...], preferred_element_type=F32)
    o_ref[...] = (ga_ref[...].astype(F32) * a + gb_ref[...].astype(F32) * b).astype(o_ref.dtype)


def _merge(oa, ob, wa, wb, proj, gate_col0):
    m, w_a = oa.shape
    w_b = ob.shape[1]
    d = wa.shape[1]
    tm = min(ROW_TILE, m)
    tn = min(COL_TILE, d)
    ga0 = gate_col0 // tn
    gb0 = (gate_col0 + d) // tn
    return pl.pallas_call(
        _merge_kernel,
        grid=(d // tn, m // tm),
        in_specs=[pl.BlockSpec((tm, w_a), lambda j, i: (i, 0)),
                  pl.BlockSpec((tm, w_b), lambda j, i: (i, 0)),
                  pl.BlockSpec((w_a, tn), lambda j, i: (0, j)),
                  pl.BlockSpec((w_b, tn), lambda j, i: (0, j)),
                  pl.BlockSpec((tm, tn), lambda j, i: (i, ga0 + j)),
                  pl.BlockSpec((tm, tn), lambda j, i: (i, gb0 + j))],
        out_specs=pl.BlockSpec((tm, tn), lambda j, i: (i, j)),
        out_shape=jax.ShapeDtypeStruct((m, d), BF16),
        compiler_params=_params(("parallel", "parallel")),
        name="merge",
    )(oa, ob, wa, wb, proj, proj)


def _outproj_kernel(x_ref, w_ref, h_ref, o_ref):
    o_ref[...] = h_ref[...] + jnp.dot(x_ref[...], w_ref[...], preferred_element_type=F32)


def _outproj(merged, w, h):
    m, d = merged.shape
    n = w.shape[1]
    tm = min(ROW_TILE, m)
    tn = min(COL_TILE, n)
    return pl.pallas_call(
        _outproj_kernel,
        grid=(n // tn, m // tm),
        in_specs=[pl.BlockSpec((tm, d), lambda j, i: (i, 0)),
                  pl.BlockSpec((d, tn), lambda j, i: (0, j)),
                  pl.BlockSpec((tm, tn), lambda j, i: (i, j))],
        out_specs=pl.BlockSpec((tm, tn), lambda j, i: (i, j)),
        out_shape=jax.ShapeDtypeStruct((m, n), F32),
        compiler_params=_params(("parallel", "parallel")),
        name="outproj",
    )(merged, w, h)


def _pack_bf16_pairs(x):
    half = x.shape[1] // 2
    lo = lax.bitcast_convert_type(x[:, :half].astype(jnp.bfloat16).astype(F32), U32)
    hi = lax.bitcast_convert_type(x[:, half:].astype(jnp.bfloat16).astype(F32), U32)
    return (lo >> 16) | (hi & jnp.uint32(0xFFFF0000))


def _unpack_bf16_pairs(u):
    lo = lax.bitcast_convert_type(u << 16, F32).astype(BF16)
    hi = lax.bitcast_convert_type(u & jnp.uint32(0xFFFF0000), F32).astype(BF16)
    return jnp.concatenate([lo, hi], axis=1)


def _router_kernel(h_ref, g_ref, wr_ref, br_ref, hf_ref, ti_ref, tw_ref):
    x = h_ref[...]
    ms = jnp.mean(x * x, axis=-1, keepdims=True)
    hf = x * lax.rsqrt(ms + EPS) * g_ref[...]
    hf_ref[...] = _pack_bf16_pairs(hf)
    logits = jnp.dot(hf, wr_ref[...], preferred_element_type=F32,
                     precision=lax.Precision.HIGHEST) + br_ref[...]
    lane = lax.broadcasted_iota(I32, logits.shape, 1).astype(F32)
    vals, idxs = [], []
    for _ in range(TOP_K):
        mx = jnp.max(logits, axis=-1, keepdims=True)
        ix = jnp.min(jnp.where(logits == mx, lane, float(LANES)), axis=-1, keepdims=True)
        vals.append(mx)
        idxs.append(ix)
        logits = jnp.where(lane == ix, -jnp.inf, logits)
    es = [jnp.exp(v - vals[0]) for v in vals]
    tot = es[0]
    for e in es[1:]:
        tot = tot + e
    ti = jnp.zeros(lane.shape, F32)
    tw = jnp.zeros(lane.shape, F32)
    for k in range(TOP_K):
        ti = jnp.where(lane == k, idxs[k], ti)
        tw = jnp.where(lane == k, es[k] / tot, tw)
    ti_ref[...] = ti.astype(I32)
    tw_ref[...] = tw


def _router(h1, g, w_router, b_router):
    m, d = h1.shape
    e = w_router.shape[1]
    assert e <= LANES
    tm = min(NORM_TILE, m)
    wr = jnp.zeros((d, LANES), F32).at[:, :e].set(w_router.astype(F32))
    br = jnp.full((1, LANES), NEG, F32).at[0, :e].set(b_router.astype(F32))
    return pl.pallas_call(
        _router_kernel,
        grid=(m // tm,),
        in_specs=[pl.BlockSpec((tm, d), lambda i: (i, 0)),
                  pl.BlockSpec((1, d), lambda i: (0, 0)),
                  pl.BlockSpec((d, LANES), lambda i: (0, 0)),
                  pl.BlockSpec((1, LANES), lambda i: (0, 0))],
        out_specs=[pl.BlockSpec((tm, d // 2), lambda i: (i, 0)),
                   pl.BlockSpec((tm, LANES), lambda i: (i, 0)),
                   pl.BlockSpec((tm, LANES), lambda i: (i, 0))],
        out_shape=[jax.ShapeDtypeStruct((m, d // 2), U32),
                   jax.ShapeDtypeStruct((m, LANES), I32),
                   jax.ShapeDtypeStruct((m, LANES), F32)],
        compiler_params=_params(("parallel",)),
        name="router",
    )(h1, g.reshape(1, d).astype(F32), wr, br)


def _row_copy(src_ref, src_row, dst_ref, dst_row, sem):
    return pltpu.make_async_copy(src_ref.at[pl.ds(src_row, 1)], dst_ref.at[pl.ds(dst_row, 1)], sem)


def _gather_kernel(nu_ref, nv_ref, idx_ref, idx_next_ref, src_ref, o_ref, buf, sems, *, n_blocks):
    b = pl.program_id(0)
    nu = nu_ref[0]

    def n_chunks(blk):
        return lax.div(nv_ref[blk] + (DMA_UNROLL - 1), DMA_UNROLL)

    def issue(rows_ref, slot, blk):
        def body(c, carry):
            for u in range(DMA_UNROLL):
                r = c * DMA_UNROLL + u
                _row_copy(src_ref, rows_ref[0, r], buf.at[slot], r, sems.at[slot]).start()
            return carry
        lax.fori_loop(0, n_chunks(blk), body, 0)

    def drain(slot, blk):
        def body(c, carry):
            for u in range(DMA_UNROLL):
                _row_copy(src_ref, 0, buf.at[slot], c * DMA_UNROLL + u, sems.at[slot]).wait()
            return carry
        lax.fori_loop(0, n_chunks(blk), body, 0)

    @pl.when(b == 0)
    def _():
        buf[...] = jnp.zeros(buf.shape, buf.dtype)
        issue(idx_ref, 0, 0)

    for slot in (0, 1):
        @pl.when(jnp.logical_and(b % 2 == slot, b < nu))
        def _():
            @pl.when(b + 1 < nu)
            def _():
                issue(idx_next_ref, 1 - slot, jnp.minimum(b + 1, n_blocks - 1))
            drain(slot, b)
            o_ref[...] = _unpack_bf16_pairs(buf[slot])

    @pl.when(b >= nu)
    def _():
        o_ref[...] = jnp.zeros(o_ref.shape, o_ref.dtype)


def _gather_rows(src, slot_rows, n_used, n_valid, tile):
    n_slots = slot_rows.shape[0]
    half = src.shape[1]
    nb = n_slots // tile
    assert tile % DMA_UNROLL == 0
    slot_blocks = slot_rows.reshape(nb, 1, tile)
    return pl.pallas_call(
        functools.partial(_gather_kernel, n_blocks=nb),
        grid=(nb,),
        in_specs=[pl.BlockSpec(memory_space=pltpu.SMEM),
                  pl.BlockSpec(memory_space=pltpu.SMEM),
                  pl.BlockSpec((None, 1, tile), lambda b: (b, 0, 0), memory_space=pltpu.SMEM),
                  pl.BlockSpec((None, 1, tile), lambda b: (jnp.minimum(b + 1, nb - 1), 0, 0),
                               memory_space=pltpu.SMEM),
                  pl.BlockSpec(memory_space=pl.ANY)],
        out_specs=pl.BlockSpec((tile, 2 * half), lambda b: (b, 0)),
        out_shape=jax.ShapeDtypeStruct((n_slots, 2 * half), BF16),
        scratch_shapes=[pltpu.VMEM((2, tile, half), src.dtype), pltpu.SemaphoreType.DMA((2,))],
        compiler_params=_params(("arbitrary",)),
        name="moe_gather",
    )(n_used, n_valid, slot_blocks, slot_blocks, src)


def _swiglu(x, wg, wu, bg, bu):
    g = jnp.dot(x, wg, preferred_element_type=F32) + bg
    u = jnp.dot(x, wu, preferred_element_type=F32) + bu
    g = jnp.minimum(g, SWIGLU_LIMIT)
    u = jnp.clip(u, -SWIGLU_LIMIT, SWIGLU_LIMIT)
    return ((u + 1.0) * (g * (1.0 / (1.0 + jnp.exp(-SWIGLU_ALPHA * g))))).astype(BF16)


def _for_valid_rows(nv, tile, o_ref, compute):
    sub = min(MOE_SUB_TILE, tile)
    for rows in range(sub, tile + 1, sub):
        @pl.when(jnp.logical_and(nv > rows - sub, nv <= rows))
        def _():
            o_ref[:rows] = compute(rows)
            if rows < tile:
                o_ref[rows:] = jnp.zeros((tile - rows, o_ref.shape[1]), o_ref.dtype)


def _moe_up_kernel(be_ref, nu_ref, nv_ref, x_ref, wg_ref, wu_ref, bg_ref, bu_ref, o_ref):
    del be_ref
    b = pl.program_id(0)

    @pl.when(b < nu_ref[0])
    def _():
        wg = wg_ref[0].astype(BF16)
        wu = wu_ref[0].astype(BF16)
        _for_valid_rows(nv_ref[b], x_ref.shape[0], o_ref,
                        lambda rows: _swiglu(x_ref[:rows], wg, wu, bg_ref[0], bu_ref[0]))

    @pl.when(b >= nu_ref[0])
    def _():
        o_ref[...] = jnp.zeros(o_ref.shape, o_ref.dtype)


def _moe_down_kernel(be_ref, nu_ref, nv_ref, a_ref, wd_ref, bd_ref, o_ref):
    del be_ref
    b = pl.program_id(0)

    @pl.when(b < nu_ref[0])
    def _():
        wd = wd_ref[0].astype(BF16)
        _for_valid_rows(nv_ref[b], a_ref.shape[0], o_ref,
                        lambda rows: jnp.dot(a_ref[:rows], wd, preferred_element_type=F32) + bd_ref[0])

    @pl.when(b >= nu_ref[0])
    def _():
        o_ref[...] = jnp.zeros(o_ref.shape, o_ref.dtype)


def _moe_experts(xs, block_e, n_used, n_valid, wg, wu, wd, bg, bu, bd, tile):
    n_slots, d = xs.shape
    e, _, ff = wg.shape
    tf = min(MOE_FF_TILE, ff)
    tn = min(MOE_DOWN_TILE, d)
    nb, nf, nn = n_slots // tile, ff // tf, d // tn

    def bb(b, nu):
        return jnp.minimum(b, nu[0] - 1)

    def frozen(b, j, last, nu):
        return jnp.where(b < nu[0], j, last)

    up_spec = pltpu.PrefetchScalarGridSpec(
        num_scalar_prefetch=3,
        grid=(nb, nf),
        in_specs=[
            pl.BlockSpec((tile, d), lambda b, f, be, nu, nv: (bb(b, nu), 0)),
            pl.BlockSpec((1, d, tf), lambda b, f, be, nu, nv: (be[bb(b, nu)], 0, frozen(b, f, nf - 1, nu))),
            pl.BlockSpec((1, d, tf), lambda b, f, be, nu, nv: (be[bb(b, nu)], 0, frozen(b, f, nf - 1, nu))),
            pl.BlockSpec((1, 1, tf), lambda b, f, be, nu, nv: (be[bb(b, nu)], 0, frozen(b, f, nf - 1, nu))),
            pl.BlockSpec((1, 1, tf), lambda b, f, be, nu, nv: (be[bb(b, nu)], 0, frozen(b, f, nf - 1, nu))),
        ],
        out_specs=pl.BlockSpec((tile, tf), lambda b, f, be, nu, nv: (b, f)),
    )
    act = pl.pallas_call(
        _moe_up_kernel,
        grid_spec=up_spec,
        out_shape=jax.ShapeDtypeStruct((n_slots, ff), BF16),
        compiler_params=_params(("arbitrary", "arbitrary")),
        name="moe_up",
    )(block_e, n_used, n_valid, xs, wg, wu, bg.reshape(e, 1, ff), bu.reshape(e, 1, ff))

    down_spec = pltpu.PrefetchScalarGridSpec(
        num_scalar_prefetch=3,
        grid=(nb, nn),
        in_specs=[
            pl.BlockSpec((tile, ff), lambda b, n, be, nu, nv: (bb(b, nu), 0)),
            pl.BlockSpec((1, ff, tn), lambda b, n, be, nu, nv: (be[bb(b, nu)], 0, frozen(b, n, nn - 1, nu))),
            pl.BlockSpec((1, 1, tn), lambda b, n, be, nu, nv: (be[bb(b, nu)], 0, frozen(b, n, nn - 1, nu))),
        ],
        out_specs=pl.BlockSpec((tile, tn), lambda b, n, be, nu, nv: (b, n)),
    )
    return pl.pallas_call(
        _moe_down_kernel,
        grid_spec=down_spec,
        out_shape=jax.ShapeDtypeStruct((n_slots, d), F32),
        compiler_params=_params(("arbitrary", "arbitrary")),
        name="moe_down",
    )(block_e, n_used, n_valid, act, wd, bd.reshape(e, 1, d))


def _combine_kernel(pos_ref, pos_next_ref, w_ref, h_ref, ys_ref, o_ref, buf, sems, *, n):
    i = pl.program_id(0)
    rows = o_ref.shape[0]

    def issue(rows_ref, slot):
        def body(r, c):
            for k in range(TOP_K):
                _row_copy(ys_ref, rows_ref[0, TOP_K * r + k], buf.at[slot, k], r, sems.at[slot]).start()
            return c
        lax.fori_loop(0, rows, body, 0, unroll=DMA_UNROLL // 2)

    def drain(slot):
        def body(r, c):
            for k in range(TOP_K):
                _row_copy(ys_ref, 0, buf.at[slot, k], r, sems.at[slot]).wait()
            return c
        lax.fori_loop(0, rows, body, 0, unroll=DMA_UNROLL // 2)

    @pl.when(i == 0)
    def _():
        issue(pos_ref, 0)

    for slot in (0, 1):
        @pl.when(i % 2 == slot)
        def _():
            @pl.when(i + 1 < n)
            def _():
                issue(pos_next_ref, 1 - slot)
            drain(slot)
            w = w_ref[...]
            acc = h_ref[...]
            for k in range(TOP_K):
                acc = acc + w[:, k:k + 1] * buf[slot, k]
            o_ref[...] = acc


def _combine(h1, tw, pos, ys, row0, n_rows):
    d = h1.shape[1]
    tc = min(COMBINE_TILE, n_rows)
    assert n_rows % tc == 0 and row0 % tc == 0
    nb, b0 = n_rows // tc, row0 // tc
    pos_blocks = pos[row0:row0 + n_rows].reshape(nb, 1, tc * TOP_K)
    return pl.pallas_call(
        functools.partial(_combine_kernel, n=nb),
        grid=(nb,),
        in_specs=[pl.BlockSpec((None, 1, tc * TOP_K), lambda i: (i, 0, 0), memory_space=pltpu.SMEM),
                  pl.BlockSpec((None, 1, tc * TOP_K), lambda i: (jnp.minimum(i + 1, nb - 1), 0, 0),
                               memory_space=pltpu.SMEM),
                  pl.BlockSpec((tc, LANES), lambda i: (b0 + i, 0)),
                  pl.BlockSpec((tc, d), lambda i: (b0 + i, 0)),
                  pl.BlockSpec(memory_space=pl.ANY)],
        out_specs=pl.BlockSpec((tc, d), lambda i: (i, 0)),
        out_shape=jax.ShapeDtypeStruct((n_rows, d), F32),
        scratch_shapes=[pltpu.VMEM((2, TOP_K, tc, d), F32), pltpu.SemaphoreType.DMA((2,))],
        compiler_params=_params(("arbitrary",)),
        name="moe_combine",
    )(pos_blocks, pos_blocks, tw, h1, ys)


def _dispatch_plan(top_i, n_experts, tile):
    n_tok = top_i.shape[0]
    nk = n_tok * TOP_K
    flat_e = top_i.reshape(-1)
    onehot = (flat_e[:, None] == jnp.arange(n_experts, dtype=I32)[None, :]).astype(I32)
    rank = jnp.take_along_axis(jnp.cumsum(onehot, axis=0), flat_e[:, None], axis=1)[:, 0] - 1
    counts = jnp.sum(onehot, axis=0)
    nblk = (counts + tile - 1) // tile
    blk_end = jnp.cumsum(nblk)
    blk_start = blk_end - nblk
    pos = blk_start[flat_e] * tile + rank
    nb_max = -(-nk // tile) + n_experts
    block_e = jnp.clip(jnp.searchsorted(blk_end, jnp.arange(nb_max, dtype=I32), side="right"),
                       0, n_experts - 1).astype(I32)
    n_used = blk_end[-1:].astype(I32)
    blocks = jnp.arange(nb_max, dtype=I32)
    n_valid = jnp.clip(counts[block_e] - (blocks - blk_start[block_e]) * tile, 0, tile)
    n_valid = jnp.where(blocks < n_used[0], n_valid, 0).astype(I32)
    return pos.astype(I32), block_e, n_used, n_valid, nb_max


def _round_up(x, m):
    return -(-x // m) * m


def kernel(x_prompt, x_sample, meta_tokens, norm_mix, w_in, qk_norm_a_q, qk_norm_a_k, lambda_q1, lambda_k1, lambda_q2, lambda_k2, subln_a, rel_bias, qk_norm_b_q, qk_norm_b_k, na_rpb, w_br_a, w_br_b, w_out, norm_ffn, w_router, b_router, w_gate, b_gate, w_up, b_up, w_down, b_down):
    d = x_prompt.shape[-1]
    w_a = w_br_a.shape[1]
    w_b = w_br_b.shape[1]
    n_experts = w_router.shape[-1]
    groups = [x_prompt, x_sample]

    row0s, seqs = [], []
    m_real = 0
    for x in groups:
        row0s.append(m_real)
        m_real += x.shape[0] * x.shape[1]
    n_seq = sum(x.shape[0] for x in groups)
    meta_row0 = m_real
    m_tok = m_real + n_seq * N_META
    m_pad = _round_up(m_tok + 1, ROW_TILE)
    h_all = jnp.concatenate(
        [x.reshape(-1, d) for x in groups]
        + [jnp.tile(meta_tokens.astype(F32), (n_seq, 1)), jnp.zeros((m_pad - m_tok, d), F32)], axis=0)

    hn = _rmsnorm_rows(h_all, norm_mix[0], BF16)
    scale_a, scale_b = HD ** -0.5 * LOG2E, HD ** -0.5 * LOG2E
    ones = lambda n: jnp.ones((n,), F32)
    gain = jnp.concatenate([
        jnp.tile(qk_norm_a_q[0].astype(F32) * scale_a, w_a // HD), jnp.tile(qk_norm_a_k[0].astype(F32), w_a // HD),
        ones(w_a),
        jnp.tile(qk_norm_b_q[0].astype(F32) * scale_b, w_b // HD), jnp.tile(qk_norm_b_k[0].astype(F32), w_b // HD),
        ones(w_b + 2 * d)]).reshape(1, -1)
    proj = _inproj(hn, w_in[0].astype(BF16), gain, w_a, w_b)

    lam_vecs = [v[0].reshape(1, HD).astype(F32) for v in (lambda_q1, lambda_k1, lambda_q2, lambda_k2)]
    subln = subln_a[0].reshape(1, 2 * HD).astype(F32)
    na_bias = _na_bias_tiles(na_rpb[0].astype(F32) * LOG2E)
    rel_bias2 = rel_bias.astype(F32) * LOG2E
    oa = jnp.zeros((m_pad, w_a), BF16)
    ob = jnp.zeros((m_pad, w_b), BF16)
    tile_cache = {}
    seq0 = 0
    for x, row0 in zip(groups, row0s):
        nb, s = x.shape[0], x.shape[1]
        mrow = meta_row0 + seq0 * N_META
        tq = min(ATTN_TQ, s)
        common = dict(w_a=w_a, n_seq=nb, seq_len=s, kv_row0=row0, meta_row0=mrow, tk=min(ATTN_TK, s))
        oa = _diff_attn(proj, oa, rel_bias2, lam_vecs, subln, tile_cache, q_row0=row0,
                        q_stride=s, tq=tq, nq=s // tq, qpos0=N_META, **common)
        oa = _diff_attn(proj, oa, rel_bias2, lam_vecs, subln, tile_cache, q_row0=mrow,
                        q_stride=N_META, tq=N_META, nq=1, qpos0=0, **common)
        ob = _na_attn(proj, ob, na_bias, w_a=w_a, w_b=w_b, n_seq=nb, seq_len=s, row0=row0, meta_row0=mrow)
        seq0 += nb
    ob = _na_meta_attn(proj, ob, w_a=w_a, w_b=w_b, n_seq=n_seq, meta_row0=meta_row0)

    merged = _merge(oa, ob, w_br_a[0].astype(BF16), w_br_b[0].astype(BF16), proj, 3 * w_a + 3 * w_b)
    h1 = _outproj(merged, w_out[0].astype(BF16), h_all)

    hf, top_i, top_w = _router(h1, norm_ffn[0], w_router[0], b_router[0])
    pos, block_e, n_used, n_valid, nb_max = _dispatch_plan(top_i[:m_tok, :TOP_K], n_experts, MOE_TILE)
    slot_rows = jnp.full((nb_max * MOE_TILE,), m_tok, I32).at[pos].set(
        jnp.repeat(jnp.arange(m_tok, dtype=I32), TOP_K))
    per_blk = MOE_TILE // MOE_SUB_TILE
    piece_valid = jnp.clip(n_valid[:, None] - jnp.arange(per_blk, dtype=I32)[None, :] * MOE_SUB_TILE,
                           0, MOE_SUB_TILE).reshape(-1)
    xs = _gather_rows(hf, slot_rows, n_used * per_blk, piece_valid, MOE_SUB_TILE)
    ys = _moe_experts(xs, block_e, n_used, n_valid, w_gate[0].astype(F32), w_up[0].astype(F32),
                      w_down[0].astype(F32), b_gate[0].astype(F32), b_up[0].astype(F32),
                      b_down[0].astype(F32), MOE_TILE)
    pos2 = pos.reshape(m_tok, TOP_K)
    outs = []
    for x, row0 in zip(groups, row0s):
        n_rows = x.shape[0] * x.shape[1]
        outs.append(_combine(h1, top_w, pos2, ys, row0, n_rows).reshape(x.shape))
    return tuple(outs)
```

```python
import functools
import math

import numpy as np
import jax
import jax.numpy as jnp
from jax import lax
from jax.experimental import pallas as pl
from jax.experimental.pallas import tpu as pltpu

F32 = jnp.float32
BF16 = jnp.bfloat16
U32 = jnp.uint32
I32 = jnp.int32

N_META = 16
GRID_W = 64
HD = 128
REL_BUCKETS = 32
REL_MAX_DIST = 128
NA_MAX_ROWS = 8
NA_COLS = 16
TOP_K = 4
SWIGLU_LIMIT = 7.0
SWIGLU_ALPHA = 1.702
EPS = 1e-6
LAMBDA_INIT = 0.8 - 0.6 * math.exp(-0.3 * 0)
NEG = -1e30
LOG2E = math.log2(math.e)

LANES = 128
VMEM_LIMIT_MB = 56
ROW_TILE = 1024
COL_TILE = 1024
NORM_TILE = 256
NA_ROWS = 8
NA_BLOCK = NA_ROWS * GRID_W
NA_WIN_ROWS = NA_ROWS + NA_MAX_ROWS
NA_WIN = NA_WIN_ROWS * GRID_W
NA_KBLOCK = NA_WIN // 4
MOE_TILE = 1536
MOE_SUB_TILE = 512
MOE_FF_TILE = 256
MOE_DOWN_TILE = 512
ATTN_TQ = 1024
ATTN_TK = 2048
COMBINE_TILE = 128
DMA_UNROLL = 8


def _params(semantics, vmem_mb=VMEM_LIMIT_MB):
    return pltpu.CompilerParams(dimension_semantics=semantics,
                                vmem_limit_bytes=vmem_mb * 2**20)


def _rmsnorm_kernel(x_ref, g_ref, o_ref):
    x = x_ref[...].astype(F32)
    ms = jnp.mean(x * x, axis=-1, keepdims=True)
    o_ref[...] = (x * lax.rsqrt(ms + EPS) * g_ref[...]).astype(o_ref.dtype)


def _rmsnorm_rows(x, g, out_dtype):
    m, d = x.shape
    tm = min(NORM_TILE, m)
    return pl.pallas_call(
        _rmsnorm_kernel,
        grid=(m // tm,),
        in_specs=[pl.BlockSpec((tm, d), lambda i: (i, 0)),
                  pl.BlockSpec((1, d), lambda i: (0, 0))],
        out_specs=pl.BlockSpec((tm, d), lambda i: (i, 0)),
        out_shape=jax.ShapeDtypeStruct((m, d), out_dtype),
        compiler_params=_params(("parallel",)),
        name="rmsnorm_rows",
    )(x, g.reshape(1, d).astype(F32))


def _inproj_kernel(x_ref, w_ref, gain_ref, o_ref, *, norm_ranges, gate_start):
    j = pl.program_id(0)
    acc = jnp.dot(x_ref[...], w_ref[...], preferred_element_type=F32)
    tn = acc.shape[1]

    is_norm = False
    for lo, hi in norm_ranges:
        is_norm = jnp.logical_or(is_norm, jnp.logical_and(j >= lo, j < hi))
    is_gate = j >= gate_start

    @pl.when(is_norm)
    def _():
        for c in range(tn // HD):
            blk = acc[:, c * HD:(c + 1) * HD]
            ms = jnp.mean(blk * blk, axis=-1, keepdims=True)
            o_ref[:, c * HD:(c + 1) * HD] = (
                blk * lax.rsqrt(ms + EPS) * gain_ref[:, c * HD:(c + 1) * HD]).astype(o_ref.dtype)

    @pl.when(is_gate)
    def _():
        o_ref[...] = (1.0 / (1.0 + jnp.exp(-acc))).astype(o_ref.dtype)

    @pl.when(jnp.logical_not(jnp.logical_or(is_norm, is_gate)))
    def _():
        o_ref[...] = acc.astype(o_ref.dtype)


def _inproj(hn, w, gain, w_a, w_b):
    m, d = hn.shape
    n = w.shape[1]
    tm = min(ROW_TILE, m)
    tn = min(COL_TILE, w_a, w_b)
    assert w_a % tn == 0 and w_b % tn == 0 and n % tn == 0 and m % tm == 0
    norm_ranges = ((0, 2 * w_a // tn), (3 * w_a // tn, (3 * w_a + 2 * w_b) // tn))
    gate_start = (3 * w_a + 3 * w_b) // tn
    kern = functools.partial(_inproj_kernel, norm_ranges=norm_ranges, gate_start=gate_start)
    return pl.pallas_call(
        kern,
        grid=(n // tn, m // tm),
        in_specs=[pl.BlockSpec((tm, d), lambda j, i: (i, 0)),
                  pl.BlockSpec((d, tn), lambda j, i: (0, j)),
                  pl.BlockSpec((1, tn), lambda j, i: (0, j))],
        out_specs=pl.BlockSpec((tm, tn), lambda j, i: (i, j)),
        out_shape=jax.ShapeDtypeStruct((m, n), BF16),
        compiler_params=_params(("parallel", "parallel")),
        name="inproj",
    )(hn, w, gain)


def _t5_bucket(rel):
    half = REL_BUCKETS // 2
    max_exact = half // 2
    sign = (rel > 0).astype(I32) * half
    n = jnp.abs(rel)
    nf = jnp.maximum(n, 1).astype(F32)
    large = max_exact + (jnp.log(nf / max_exact) / math.log(REL_MAX_DIST / max_exact)
                         * (half - max_exact)).astype(I32)
    large = jnp.minimum(large, half - 1)
    return sign + jnp.where(n < max_exact, n, large)


def _bias_tile_plan(qpos0, nq, tq, nk, tk):
    keys, ids = [], np.zeros((nq, nk), np.int32)
    for i in range(nq):
        for j in range(nk):
            d = (N_META + j * tk) - (qpos0 + i * tq)
            if d - (tq - 1) >= REL_MAX_DIST:
                key = ("far", REL_MAX_DIST)
            elif d + (tk - 1) <= -REL_MAX_DIST:
                key = ("far", -REL_MAX_DIST)
            else:
                key = ("near", d)
            if key not in keys:
                keys.append(key)
            ids[i, j] = keys.index(key)
    return tuple(keys), ids.reshape(-1)


def _bias_tiles(rel_bias, keys, tq, tk):
    h = rel_bias.shape[1]
    blocked = tq % LANES == 0 and tk % LANES == 0 and all(v % LANES == 0 for _, v in keys)
    if not blocked:
        ramp = np.arange(tk)[None, :] - np.arange(tq)[:, None]
        rel = np.stack([np.full((tq, tk), v) if kind == "far" else v + ramp for kind, v in keys])
        b = rel_bias[_t5_bucket(jnp.asarray(rel.astype(np.int32)))]
        return jnp.transpose(b, (3, 0, 1, 2)).astype(F32)

    def const(v):
        return rel_bias[_t5_bucket(jnp.full((1,), v, I32))].reshape(h, 1, 1, 1, 1)

    r = np.arange(LANES)
    rel3 = np.stack([LANES * dl + r[None, :] - r[:, None] for dl in (-1, 0, 1)]).astype(np.int32)
    f3 = jnp.transpose(rel_bias[_t5_bucket(jnp.asarray(rel3))], (3, 0, 1, 2))
    na, nc = tq // LANES, tk // LANES
    tiles = []
    for kind, v in keys:
        if kind == "far":
            tiles.append(jnp.broadcast_to(const(v).reshape(h, 1, 1), (h, tq, tk)))
            continue
        delta = v // LANES + np.arange(nc)[None, :] - np.arange(na)[:, None]
        t5 = jnp.where(jnp.asarray(delta < 0)[None, :, None, :, None], const(-REL_MAX_DIST), const(REL_MAX_DIST))
        for dl in (-1, 0, 1):
            t5 = jnp.where(jnp.asarray(delta == dl)[None, :, None, :, None], f3[:, dl + 1][:, None, :, None, :], t5)
        tiles.append(t5.reshape(h, tq, tk))
    return jnp.stack(tiles, axis=1).astype(F32)


def _meta_bias_tiles(rel_bias, qpos0, nq, tq):
    n = 1
    while n < nq and qpos0 + (n - 1) * tq - (N_META - 1) < REL_MAX_DIST:
        n += 1
    qpos = qpos0 + np.arange(n * tq).reshape(n, tq)
    rel = np.arange(N_META)[None, None, :] - qpos[:, :, None]
    b = jnp.transpose(rel_bias[_t5_bucket(jnp.asarray(rel.astype(np.int32)))], (3, 0, 1, 2)).astype(F32)
    pad = jnp.full(b.shape[:3] + (LANES - N_META,), NEG, F32)
    return jnp.concatenate([b, pad], axis=-1), n


def _softmax_step(s, m_ref, l_ref, a_ref, v, first):
    m_cur = jnp.max(s, axis=-1, keepdims=True)
    if first:
        m_new = m_cur
        p = jnp.exp2(s - m_new)
        l_ref[...] = jnp.sum(p, axis=-1, keepdims=True)
        a_ref[...] = jnp.dot(p.astype(v.dtype), v, preferred_element_type=F32)
    else:
        m_prev = m_ref[...]
        m_new = jnp.maximum(m_prev, m_cur)
        alpha = jnp.exp2(m_prev - m_new)
        p = jnp.exp2(s - m_new)
        l_ref[...] = alpha * l_ref[...] + jnp.sum(p, axis=-1, keepdims=True)
        a_ref[...] = alpha * a_ref[...] + jnp.dot(p.astype(v.dtype), v, preferred_element_type=F32)
    m_ref[...] = m_new


def _qk(q, k):
    return lax.dot_general(q, k, (((1,), (1,)), ((), ())), preferred_element_type=F32)


def _pad_meta(x):
    return jnp.concatenate([x, jnp.zeros((LANES - N_META, x.shape[1]), x.dtype)], axis=0)


def _diff_attn_kernel(tid_ref, q_ref, k_ref, v_ref, km_ref, vm_ref, bias_ref, biasm_ref,
                      lq1_ref, lk1_ref, lq2_ref, lk2_ref, sub_ref, prev_ref, o_ref,
                      m1, l1, a1, m2, l2, a2, *, nk):
    del tid_ref, prev_ref
    j = pl.program_id(3)
    q = q_ref[...]
    q1, q2 = q[:, :HD], q[:, HD:]

    @pl.when(j == 0)
    def _():
        km = _pad_meta(km_ref[...])
        vm = _pad_meta(vm_ref[...])
        bm = biasm_ref[0, 0]
        _softmax_step(_qk(q1, km[:, :HD]) + bm, m1, l1, a1, vm, True)
        _softmax_step(_qk(q2, km[:, HD:]) + bm, m2, l2, a2, vm, True)

    k = k_ref[...]
    v = v_ref[...]
    b = bias_ref[0, 0]
    _softmax_step(_qk(q1, k[:, :HD]) + b, m1, l1, a1, v, False)
    _softmax_step(_qk(q2, k[:, HD:]) + b, m2, l2, a2, v, False)

    @pl.when(j == nk - 1)
    def _():
        lam = (jnp.exp(jnp.sum(lq1_ref[...] * lk1_ref[...], axis=-1, keepdims=True))
               - jnp.exp(jnp.sum(lq2_ref[...] * lk2_ref[...], axis=-1, keepdims=True))
               + LAMBDA_INIT)
        o = a1[...] / l1[...] - lam * (a2[...] / l2[...])
        ms = jnp.mean(o * o, axis=-1, keepdims=True)
        o_ref[...] = (o * lax.rsqrt(ms + EPS) * sub_ref[...] * (1.0 - LAMBDA_INIT)).astype(o_ref.dtype)


def _diff_attn(proj, oa, rel_bias, lam_vecs, subln, tile_cache, *, w_a, n_seq, seq_len, kv_row0,
               q_row0, q_stride, tq, nq, qpos0, meta_row0, tk):
    h_a = w_a // (2 * HD)
    nk = seq_len // tk
    assert seq_len % tk == 0 and kv_row0 % tk == 0 and q_row0 % tq == 0 and q_stride % tq == 0
    assert meta_row0 % N_META == 0
    keys, ids = _bias_tile_plan(qpos0, nq, tq, nk, tk)
    if (keys, tq, tk) not in tile_cache:
        tile_cache[(keys, tq, tk)] = _bias_tiles(rel_bias, keys, tq, tk)
    tiles = tile_cache[(keys, tq, tk)]
    biasm, n_bm = _meta_bias_tiles(rel_bias, qpos0, nq, tq)
    qb0, qbs = q_row0 // tq, q_stride // tq
    kb0, kbs = kv_row0 // tk, seq_len // tk
    mb0 = meta_row0 // N_META
    kcol, vcol = w_a // (2 * HD), 2 * w_a // (2 * HD)
    dv = 2 * HD

    def qmap(b, h, i, j, t):
        return (qb0 + b * qbs + i, h)

    vec = pl.BlockSpec((1, HD), lambda b, h, i, j, t: (0, 0))
    grid_spec = pltpu.PrefetchScalarGridSpec(
        num_scalar_prefetch=1,
        grid=(n_seq, h_a, nq, nk),
        in_specs=[
            pl.BlockSpec((tq, dv), qmap),
            pl.BlockSpec((tk, dv), lambda b, h, i, j, t: (kb0 + b * kbs + j, kcol + h)),
            pl.BlockSpec((tk, dv), lambda b, h, i, j, t: (kb0 + b * kbs + j, vcol + h)),
            pl.BlockSpec((N_META, dv), lambda b, h, i, j, t: (mb0 + b, kcol + h)),
            pl.BlockSpec((N_META, dv), lambda b, h, i, j, t: (mb0 + b, vcol + h)),
            pl.BlockSpec((1, 1, tq, tk), lambda b, h, i, j, t: (h, t[i * nk + j], 0, 0)),
            pl.BlockSpec((1, 1, tq, LANES), lambda b, h, i, j, t: (h, jnp.minimum(i, n_bm - 1), 0, 0)),
            vec, vec, vec, vec,
            pl.BlockSpec((1, dv), lambda b, h, i, j, t: (0, 0)),
            pl.BlockSpec(memory_space=pl.ANY),
        ],
        out_specs=pl.BlockSpec((tq, dv), qmap),
        scratch_shapes=[pltpu.VMEM((tq, 1), F32), pltpu.VMEM((tq, 1), F32), pltpu.VMEM((tq, dv), F32),
                        pltpu.VMEM((tq, 1), F32), pltpu.VMEM((tq, 1), F32), pltpu.VMEM((tq, dv), F32)],
    )
    return pl.pallas_call(
        functools.partial(_diff_attn_kernel, nk=nk),
        grid_spec=grid_spec,
        out_shape=jax.ShapeDtypeStruct(oa.shape, oa.dtype),
        input_output_aliases={13: 0},
        compiler_params=_params(("parallel", "parallel", "parallel", "arbitrary")),
        name=f"diff_attn_q{tq}",
    )(jnp.asarray(ids), proj, proj, proj, proj, proj, tiles, biasm, *lam_vecs, subln, oa)


def _na_plan():
    rows_c = 3 * NA_ROWS
    a = np.arange(NA_ROWS)
    kl = np.arange(NA_WIN_ROWS)
    ridx = np.zeros((3, NA_ROWS, NA_WIN_ROWS), np.int32)
    rvalid = np.zeros((3, NA_ROWS, NA_WIN_ROWS), bool)
    for v in range(3):
        r = v * NA_ROWS + a
        rs = np.clip(r - NA_MAX_ROWS // 2, 0, rows_c - NA_MAX_ROWS)
        kr = v * NA_ROWS - NA_MAX_ROWS // 2 + kl
        off = kr[None, :] - r[:, None]
        ok = (kr[None, :] >= rs[:, None]) & (kr[None, :] < rs[:, None] + NA_MAX_ROWS)
        ok &= (kr[None, :] >= 0) & (kr[None, :] < rows_c)
        ridx[v] = np.clip(off + NA_MAX_ROWS - 1, 0, 2 * NA_MAX_ROWS - 2)
        rvalid[v] = ok
    c = np.arange(GRID_W)
    cs = np.clip(c - NA_COLS // 2, 0, GRID_W - NA_COLS)
    coff = c[None, :] - c[:, None]
    cvalid = (c[None, :] >= cs[:, None]) & (c[None, :] < cs[:, None] + NA_COLS)
    cidx = np.clip(coff + NA_COLS - 1, 0, 2 * NA_COLS - 2).astype(np.int32)
    return ridx, rvalid, cidx, cvalid


def _na_bias_tiles(rpb):
    ridx, rvalid, cidx, cvalid = _na_plan()
    h = rpb.shape[0]
    t1 = rpb[:, :, cidx]
    t2 = t1[:, ridx]
    valid = rvalid[:, :, :, None, None] & cvalid[None, None, None]
    t2 = jnp.where(jnp.asarray(valid)[None], t2, NEG)
    t2 = jnp.transpose(t2, (0, 1, 2, 4, 3, 5))
    return t2.reshape(h, 3, NA_BLOCK, NA_WIN).astype(F32)


def _na_kernel(q_ref, k0_ref, k1_ref, k2_ref, k3_ref, v0_ref, v1_ref, v2_ref, v3_ref, km_ref, vm_ref,
               bias_ref, prev_ref, o_ref):
    del prev_ref
    q = q_ref[...]
    k = jnp.concatenate([k0_ref[...], k1_ref[...], k2_ref[...], k3_ref[...]], axis=0)
    v = jnp.concatenate([v0_ref[...], v1_ref[...], v2_ref[...], v3_ref[...]], axis=0)
    s = _qk(q, k) + bias_ref[0, 0]
    km = _pad_meta(km_ref[...])
    vm = _pad_meta(vm_ref[...])
    lane = lax.broadcasted_iota(I32, (q.shape[0], LANES), 1)
    sm = jnp.where(lane < N_META, _qk(q, km), NEG)
    m = jnp.maximum(jnp.max(s, axis=-1, keepdims=True), jnp.max(sm, axis=-1, keepdims=True))
    p = jnp.exp2(s - m)
    pm = jnp.exp2(sm - m)
    l = jnp.sum(p, axis=-1, keepdims=True) + jnp.sum(pm, axis=-1, keepdims=True)
    o = (jnp.dot(p.astype(v.dtype), v, preferred_element_type=F32)
         + jnp.dot(pm.astype(vm.dtype), vm, preferred_element_type=F32))
    o_ref[...] = (o / l).astype(o_ref.dtype)


def _na_attn(proj, ob, bias, *, w_a, w_b, n_seq, seq_len, row0, meta_row0):
    h_b = w_b // HD
    nblk = seq_len // NA_BLOCK
    assert seq_len % NA_BLOCK == 0 and nblk >= 2 and row0 % NA_BLOCK == 0
    rb0 = row0 // NA_BLOCK
    mb0 = meta_row0 // N_META
    qcol, kcol, vcol = 3 * w_a // HD, (3 * w_a + w_b) // HD, (3 * w_a + 2 * w_b) // HD

    per_q = NA_BLOCK // NA_KBLOCK
    n_kb = nblk * per_q

    def kblk(col, t):
        def index(h, b, i):
            return (rb0 * per_q + b * n_kb + jnp.clip(i * per_q - 1 + t, 0, n_kb - 1), col + h)
        return pl.BlockSpec((NA_KBLOCK, HD), index)

    def variant(i):
        return jnp.where(i == 0, 0, jnp.where(i == nblk - 1, 2, 1))

    n_win = NA_WIN // NA_KBLOCK
    return pl.pallas_call(
        _na_kernel,
        grid=(h_b, n_seq, nblk),
        in_specs=[pl.BlockSpec((NA_BLOCK, HD), lambda h, b, i: (rb0 + b * nblk + i, qcol + h))]
                 + [kblk(kcol, t) for t in range(n_win)] + [kblk(vcol, t) for t in range(n_win)]
                 + [pl.BlockSpec((N_META, HD), lambda h, b, i: (mb0 + b, kcol + h)),
                    pl.BlockSpec((N_META, HD), lambda h, b, i: (mb0 + b, vcol + h)),
                    pl.BlockSpec((1, 1, NA_BLOCK, NA_WIN), lambda h, b, i: (h, variant(i), 0, 0)),
                    pl.BlockSpec(memory_space=pl.ANY)],
        out_specs=pl.BlockSpec((NA_BLOCK, HD), lambda h, b, i: (rb0 + b * nblk + i, h)),
        out_shape=jax.ShapeDtypeStruct(ob.shape, ob.dtype),
        input_output_aliases={2 * n_win + 4: 0},
        compiler_params=_params(("parallel", "parallel", "parallel")),
        name="na_attn",
    )(*([proj] * (2 * n_win + 3)), bias, ob)


def _na_meta_kernel(q_ref, km_ref, vm_ref, prev_ref, o_ref):
    del prev_ref
    q = jnp.concatenate([q_ref[...], jnp.zeros((LANES - N_META, HD), q_ref.dtype)], axis=0)
    km = _pad_meta(km_ref[...])
    vm = _pad_meta(vm_ref[...])
    lane = lax.broadcasted_iota(I32, (LANES, LANES), 1)
    s = jnp.where(lane < N_META, _qk(q, km), NEG)
    m = jnp.max(s, axis=-1, keepdims=True)
    p = jnp.exp2(s - m)
    o = jnp.dot(p.astype(vm.dtype), vm, preferred_element_type=F32) / jnp.sum(p, axis=-1, keepdims=True)
    o_ref[...] = o[:N_META].astype(o_ref.dtype)


def _na_meta_attn(proj, ob, *, w_a, w_b, n_seq, meta_row0):
    h_b = w_b // HD
    mb0 = meta_row0 // N_META
    qcol, kcol, vcol = 3 * w_a // HD, (3 * w_a + w_b) // HD, (3 * w_a + 2 * w_b) // HD
    return pl.pallas_call(
        _na_meta_kernel,
        grid=(n_seq, h_b),
        in_specs=[pl.BlockSpec((N_META, HD), lambda s, h: (mb0 + s, qcol + h)),
                  pl.BlockSpec((N_META, HD), lambda s, h: (mb0 + s, kcol + h)),
                  pl.BlockSpec((N_META, HD), lambda s, h: (mb0 + s, vcol + h)),
                  pl.BlockSpec(memory_space=pl.ANY)],
        out_specs=pl.BlockSpec((N_META, HD), lambda s, h: (mb0 + s, h)),
        out_shape=jax.ShapeDtypeStruct(ob.shape, ob.dtype),
        input_output_aliases={3: 0},
        compiler_params=_params(("parallel", "parallel")),
        name="na_meta_attn",
    )(proj, proj, proj, ob)


def _merge_kernel(oa_ref, ob_ref, wa_ref, wb_ref, ga_ref, gb_ref, o_ref):
    a = jnp.dot(oa_ref[...], wa_ref[...], preferred_element_type=F32)
    b = jnp.dot(ob_ref[...], wb_ref[...], preferred_element_type=F32)
    o_ref[...] = (ga_ref[...].astype(F32) * a + gb_ref[...].astype(F32) * b).astype(o_ref.dtype)


def _merge(oa, ob, wa, wb, proj, gate_col0):
    m, w_a = oa.shape
    w_b = ob.shape[1]
    d = wa.shape[1]
    tm = min(ROW_TILE, m)
    tn = min(COL_TILE, d)
    ga0 = gate_col0 // tn
    gb0 = (gate_col0 + d) // tn
    return pl.pallas_call(
        _merge_kernel,
        grid=(d // tn, m // tm),
        in_specs=[pl.BlockSpec((tm, w_a), lambda j, i: (i, 0)),
                  pl.BlockSpec((tm, w_b), lambda j, i: (i, 0)),
                  pl.BlockSpec((w_a, tn), lambda j, i: (0, j)),
                  pl.BlockSpec((w_b, tn), lambda j, i: (0, j)),
                  pl.BlockSpec((tm, tn), lambda j, i: (i, ga0 + j)),
                  pl.BlockSpec((tm, tn), lambda j, i: (i, gb0 + j))],
        out_specs=pl.BlockSpec((tm, tn), lambda j, i: (i, j)),
        out_shape=jax.ShapeDtypeStruct((m, d), BF16),
        compiler_params=_params(("parallel", "parallel")),
        name="merge",
    )(oa, ob, wa, wb, proj, proj)


def _outproj_kernel(x_ref, w_ref, h_ref, o_ref):
    o_ref[...] = h_ref[...] + jnp.dot(x_ref[...], w_ref[...], preferred_element_type=F32)


def _outproj(merged, w, h):
    m, d = merged.shape
    n = w.shape[1]
    tm = min(ROW_TILE, m)
    tn = min(COL_TILE, n)
    return pl.pallas_call(
        _outproj_kernel,
        grid=(n // tn, m // tm),
        in_specs=[pl.BlockSpec((tm, d), lambda j, i: (i, 0)),
                  pl.BlockSpec((d, tn), lambda j, i: (0, j)),
                  pl.BlockSpec((tm, tn), lambda j, i: (i, j))],
        out_specs=pl.BlockSpec((tm, tn), lambda j, i: (i, j)),
        out_shape=jax.ShapeDtypeStruct((m, n), F32),
        compiler_params=_params(("parallel", "parallel")),
        name="outproj",
    )(merged, w, h)


def _pack_bf16_pairs(x):
    half = x.shape[1] // 2
    lo = lax.bitcast_convert_type(x[:, :half].astype(jnp.bfloat16).astype(F32), U32)
    hi = lax.bitcast_convert_type(x[:, half:].astype(jnp.bfloat16).astype(F32), U32)
    return (lo >> 16) | (hi & jnp.uint32(0xFFFF0000))


def _unpack_bf16_pairs(u):
    lo = lax.bitcast_convert_type(u << 16, F32).astype(BF16)
    hi = lax.bitcast_convert_type(u & jnp.uint32(0xFFFF0000), F32).astype(BF16)
    return jnp.concatenate([lo, hi], axis=1)


def _router_kernel(h_ref, g_ref, wr_ref, br_ref, hf_ref, ti_ref, tw_ref):
    x = h_ref[...]
    ms = jnp.mean(x * x, axis=-1, keepdims=True)
    hf = x * lax.rsqrt(ms + EPS) * g_ref[...]
    hf_ref[...] = _pack_bf16_pairs(hf)
    logits = jnp.dot(hf, wr_ref[...], preferred_element_type=F32,
                     precision=lax.Precision.HIGHEST) + br_ref[...]
    lane = lax.broadcasted_iota(I32, logits.shape, 1).astype(F32)
    vals, idxs = [], []
    for _ in range(TOP_K):
        mx = jnp.max(logits, axis=-1, keepdims=True)
        ix = jnp.min(jnp.where(logits == mx, lane, float(LANES)), axis=-1, keepdims=True)
        vals.append(mx)
        idxs.append(ix)
        logits = jnp.where(lane == ix, -jnp.inf, logits)
    es = [jnp.exp(v - vals[0]) for v in vals]
    tot = es[0]
    for e in es[1:]:
        tot = tot + e
    ti = jnp.zeros(lane.shape, F32)
    tw = jnp.zeros(lane.shape, F32)
    for k in range(TOP_K):
        ti = jnp.where(lane == k, idxs[k], ti)
        tw = jnp.where(lane == k, es[k] / tot, tw)
    ti_ref[...] = ti.astype(I32)
    tw_ref[...] = tw


def _router(h1, g, w_router, b_router):
    m, d = h1.shape
    e = w_router.shape[1]
    assert e <= LANES
    tm = min(NORM_TILE, m)
    wr = jnp.zeros((d, LANES), F32).at[:, :e].set(w_router.astype(F32))
    br = jnp.full((1, LANES), NEG, F32).at[0, :e].set(b_router.astype(F32))
    return pl.pallas_call(
        _router_kernel,
        grid=(m // tm,),
        in_specs=[pl.BlockSpec((tm, d), lambda i: (i, 0)),
                  pl.BlockSpec((1, d), lambda i: (0, 0)),
                  pl.BlockSpec((d, LANES), lambda i: (0, 0)),
                  pl.BlockSpec((1, LANES), lambda i: (0, 0))],
        out_specs=[pl.BlockSpec((tm, d // 2), lambda i: (i, 0)),
                   pl.BlockSpec((tm, LANES), lambda i: (i, 0)),
                   pl.BlockSpec((tm, LANES), lambda i: (i, 0))],
        out_shape=[jax.ShapeDtypeStruct((m, d // 2), U32),
                   jax.ShapeDtypeStruct((m, LANES), I32),
                   jax.ShapeDtypeStruct((m, LANES), F32)],
        compiler_params=_params(("parallel",)),
        name="router",
    )(h1, g.reshape(1, d).astype(F32), wr, br)


def _row_copy(src_ref, src_row, dst_ref, dst_row, sem):
    return pltpu.make_async_copy(src_ref.at[pl.ds(src_row, 1)], dst_ref.at[pl.ds(dst_row, 1)], sem)


def _gather_kernel(nu_ref, nv_ref, idx_ref, idx_next_ref, src_ref, o_ref, buf, sems, *, n_blocks):
    b = pl.program_id(0)
    nu = nu_ref[0]

    def n_chunks(blk):
        return lax.div(nv_ref[blk] + (DMA_UNROLL - 1), DMA_UNROLL)

    def issue(rows_ref, slot, blk):
        def body(c, carry):
            for u in range(DMA_UNROLL):
                r = c * DMA_UNROLL + u
                _row_copy(src_ref, rows_ref[0, r], buf.at[slot], r, sems.at[slot]).start()
            return carry
        lax.fori_loop(0, n_chunks(blk), body, 0)

    def drain(slot, blk):
        def body(c, carry):
            for u in range(DMA_UNROLL):
                _row_copy(src_ref, 0, buf.at[slot], c * DMA_UNROLL + u, sems.at[slot]).wait()
            return carry
        lax.fori_loop(0, n_chunks(blk), body, 0)

    @pl.when(b == 0)
    def _():
        buf[...] = jnp.zeros(buf.shape, buf.dtype)
        issue(idx_ref, 0, 0)

    for slot in (0, 1):
        @pl.when(jnp.logical_and(b % 2 == slot, b < nu))
        def _():
            @pl.when(b + 1 < nu)
            def _():
                issue(idx_next_ref, 1 - slot, jnp.minimum(b + 1, n_blocks - 1))
            drain(slot, b)
            o_ref[...] = _unpack_bf16_pairs(buf[slot])

    @pl.when(b >= nu)
    def _():
        o_ref[...] = jnp.zeros(o_ref.shape, o_ref.dtype)


def _gather_rows(src, slot_rows, n_used, n_valid, tile):
    n_slots = slot_rows.shape[0]
    half = src.shape[1]
    nb = n_slots // tile
    assert tile % DMA_UNROLL == 0
    slot_blocks = slot_rows.reshape(nb, 1, tile)
    return pl.pallas_call(
        functools.partial(_gather_kernel, n_blocks=nb),
        grid=(nb,),
        in_specs=[pl.BlockSpec(memory_space=pltpu.SMEM),
                  pl.BlockSpec(memory_space=pltpu.SMEM),
                  pl.BlockSpec((None, 1, tile), lambda b: (b, 0, 0), memory_space=pltpu.SMEM),
                  pl.BlockSpec((None, 1, tile), lambda b: (jnp.minimum(b + 1, nb - 1), 0, 0),
                               memory_space=pltpu.SMEM),
                  pl.BlockSpec(memory_space=pl.ANY)],
        out_specs=pl.BlockSpec((tile, 2 * half), lambda b: (b, 0)),
        out_shape=jax.ShapeDtypeStruct((n_slots, 2 * half), BF16),
        scratch_shapes=[pltpu.VMEM((2, tile, half), src.dtype), pltpu.SemaphoreType.DMA((2,))],
        compiler_params=_params(("arbitrary",)),
        name="moe_gather",
    )(n_used, n_valid, slot_blocks, slot_blocks, src)


def _swiglu(x, wg, wu, bg, bu):
    g = jnp.dot(x, wg, preferred_element_type=F32) + bg
    u = jnp.dot(x, wu, preferred_element_type=F32) + bu
    g = jnp.minimum(g, SWIGLU_LIMIT)
    u = jnp.clip(u, -SWIGLU_LIMIT, SWIGLU_LIMIT)
    return ((u + 1.0) * (g * (1.0 / (1.0 + jnp.exp(-SWIGLU_ALPHA * g))))).astype(BF16)


def _for_valid_rows(nv, tile, o_ref, compute):
    sub = min(MOE_SUB_TILE, tile)
    for rows in range(sub, tile + 1, sub):
        @pl.when(jnp.logical_and(nv > rows - sub, nv <= rows))
        def _():
            o_ref[:rows] = compute(rows)
            if rows < tile:
                o_ref[rows:] = jnp.zeros((tile - rows, o_ref.shape[1]), o_ref.dtype)


def _moe_up_kernel(be_ref, nu_ref, nv_ref, x_ref, wg_ref, wu_ref, bg_ref, bu_ref, o_ref):
    del be_ref
    b = pl.program_id(0)

    @pl.when(b < nu_ref[0])
    def _():
        wg = wg_ref[0].astype(BF16)
        wu = wu_ref[0].astype(BF16)
        _for_valid_rows(nv_ref[b], x_ref.shape[0], o_ref,
                        lambda rows: _swiglu(x_ref[:rows], wg, wu, bg_ref[0], bu_ref[0]))

    @pl.when(b >= nu_ref[0])
    def _():
        o_ref[...] = jnp.zeros(o_ref.shape, o_ref.dtype)


def _moe_down_kernel(be_ref, nu_ref, nv_ref, a_ref, wd_ref, bd_ref, o_ref):
    del be_ref
    b = pl.program_id(0)

    @pl.when(b < nu_ref[0])
    def _():
        wd = wd_ref[0].astype(BF16)
        _for_valid_rows(nv_ref[b], a_ref.shape[0], o_ref,
                        lambda rows: jnp.dot(a_ref[:rows], wd, preferred_element_type=F32) + bd_ref[0])

    @pl.when(b >= nu_ref[0])
    def _():
        o_ref[...] = jnp.zeros(o_ref.shape, o_ref.dtype)


def _moe_experts(xs, block_e, n_used, n_valid, wg, wu, wd, bg, bu, bd, tile):
    n_slots, d = xs.shape
    e, _, ff = wg.shape
    tf = min(MOE_FF_TILE, ff)
    tn = min(MOE_DOWN_TILE, d)
    nb, nf, nn = n_slots // tile, ff // tf, d // tn

    def bb(b, nu):
        return jnp.minimum(b, nu[0] - 1)

    def frozen(b, j, last, nu):
        return jnp.where(b < nu[0], j, last)

    up_spec = pltpu.PrefetchScalarGridSpec(
        num_scalar_prefetch=3,
        grid=(nb, nf),
        in_specs=[
            pl.BlockSpec((tile, d), lambda b, f, be, nu, nv: (bb(b, nu), 0)),
            pl.BlockSpec((1, d, tf), lambda b, f, be, nu, nv: (be[bb(b, nu)], 0, frozen(b, f, nf - 1, nu))),
            pl.BlockSpec((1, d, tf), lambda b, f, be, nu, nv: (be[bb(b, nu)], 0, frozen(b, f, nf - 1, nu))),
            pl.BlockSpec((1, 1, tf), lambda b, f, be, nu, nv: (be[bb(b, nu)], 0, frozen(b, f, nf - 1, nu))),
            pl.BlockSpec((1, 1, tf), lambda b, f, be, nu, nv: (be[bb(b, nu)], 0, frozen(b, f, nf - 1, nu))),
        ],
        out_specs=pl.BlockSpec((tile, tf), lambda b, f, be, nu, nv: (b, f)),
    )
    act = pl.pallas_call(
        _moe_up_kernel,
        grid_spec=up_spec,
        out_shape=jax.ShapeDtypeStruct((n_slots, ff), BF16),
        compiler_params=_params(("arbitrary", "arbitrary")),
        name="moe_up",
    )(block_e, n_used, n_valid, xs, wg, wu, bg.reshape(e, 1, ff), bu.reshape(e, 1, ff))

    down_spec = pltpu.PrefetchScalarGridSpec(
        num_scalar_prefetch=3,
        grid=(nb, nn),
        in_specs=[
            pl.BlockSpec((tile, ff), lambda b, n, be, nu, nv: (bb(b, nu), 0)),
            pl.BlockSpec((1, ff, tn), lambda b, n, be, nu, nv: (be[bb(b, nu)], 0, frozen(b, n, nn - 1, nu))),
            pl.BlockSpec((1, 1, tn), lambda b, n, be, nu, nv: (be[bb(b, nu)], 0, frozen(b, n, nn - 1, nu))),
        ],
        out_specs=pl.BlockSpec((tile, tn), lambda b, n, be, nu, nv: (b, n)),
    )
    return pl.pallas_call(
        _moe_down_kernel,
        grid_spec=down_spec,
        out_shape=jax.ShapeDtypeStruct((n_slots, d), F32),
        compiler_params=_params(("arbitrary", "arbitrary")),
        name="moe_down",
    )(block_e, n_used, n_valid, act, wd, bd.reshape(e, 1, d))


def _combine_kernel(pos_ref, pos_next_ref, w_ref, h_ref, ys_ref, o_ref, buf, sems, *, n):
    i = pl.program_id(0)
    rows = o_ref.shape[0]

    def issue(rows_ref, slot):
        def body(r, c):
            for k in range(TOP_K):
                _row_copy(ys_ref, rows_ref[0, TOP_K * r + k], buf.at[slot, k], r, sems.at[slot]).start()
            return c
        lax.fori_loop(0, rows, body, 0, unroll=DMA_UNROLL // 2)

    def drain(slot):
        def body(r, c):
            for k in range(TOP_K):
                _row_copy(ys_ref, 0, buf.at[slot, k], r, sems.at[slot]).wait()
            return c
        lax.fori_loop(0, rows, body, 0, unroll=DMA_UNROLL // 2)

    @pl.when(i == 0)
    def _():
        issue(pos_ref, 0)

    for slot in (0, 1):
        @pl.when(i % 2 == slot)
        def _():
            @pl.when(i + 1 < n)
            def _():
                issue(pos_next_ref, 1 - slot)
            drain(slot)
            w = w_ref[...]
            acc = h_ref[...]
            for k in range(TOP_K):
                acc = acc + w[:, k:k + 1] * buf[slot, k]
            o_ref[...] = acc


def _combine(h1, tw, pos, ys, row0, n_rows):
    d = h1.shape[1]
    tc = min(COMBINE_TILE, n_rows)
    assert n_rows % tc == 0 and row0 % tc == 0
    nb, b0 = n_rows // tc, row0 // tc
    pos_blocks = pos[row0:row0 + n_rows].reshape(nb, 1, tc * TOP_K)
    return pl.pallas_call(
        functools.partial(_combine_kernel, n=nb),
        grid=(nb,),
        in_specs=[pl.BlockSpec((None, 1, tc * TOP_K), lambda i: (i, 0, 0), memory_space=pltpu.SMEM),
                  pl.BlockSpec((None, 1, tc * TOP_K), lambda i: (jnp.minimum(i + 1, nb - 1), 0, 0),
                               memory_space=pltpu.SMEM),
                  pl.BlockSpec((tc, LANES), lambda i: (b0 + i, 0)),
                  pl.BlockSpec((tc, d), lambda i: (b0 + i, 0)),
                  pl.BlockSpec(memory_space=pl.ANY)],
        out_specs=pl.BlockSpec((tc, d), lambda i: (i, 0)),
        out_shape=jax.ShapeDtypeStruct((n_rows, d), F32),
        scratch_shapes=[pltpu.VMEM((2, TOP_K, tc, d), F32), pltpu.SemaphoreType.DMA((2,))],
        compiler_params=_params(("arbitrary",)),
        name="moe_combine",
    )(pos_blocks, pos_blocks, tw, h1, ys)


def _dispatch_plan(top_i, n_experts, tile):
    n_tok = top_i.shape[0]
    nk = n_tok * TOP_K
    flat_e = top_i.reshape(-1)
    onehot = (flat_e[:, None] == jnp.arange(n_experts, dtype=I32)[None, :]).astype(I32)
    rank = jnp.take_along_axis(jnp.cumsum(onehot, axis=0), flat_e[:, None], axis=1)[:, 0] - 1
    counts = jnp.sum(onehot, axis=0)
    nblk = (counts + tile - 1) // tile
    blk_end = jnp.cumsum(nblk)
    blk_start = blk_end - nblk
    pos = blk_start[flat_e] * tile + rank
    nb_max = -(-nk // tile) + n_experts
    block_e = jnp.clip(jnp.searchsorted(blk_end, jnp.arange(nb_max, dtype=I32), side="right"),
                       0, n_experts - 1).astype(I32)
    n_used = blk_end[-1:].astype(I32)
    blocks = jnp.arange(nb_max, dtype=I32)
    n_valid = jnp.clip(counts[block_e] - (blocks - blk_start[block_e]) * tile, 0, tile)
    n_valid = jnp.where(blocks < n_used[0], n_valid, 0).astype(I32)
    return pos.astype(I32), block_e, n_used, n_valid, nb_max


def _round_up(x, m):
    return -(-x // m) * m


def kernel(x_prompt, x_sample, meta_tokens, norm_mix, w_in, qk_norm_a_q, qk_norm_a_k, lambda_q1, lambda_k1, lambda_q2, lambda_k2, subln_a, rel_bias, qk_norm_b_q, qk_norm_b_k, na_rpb, w_br_a, w_br_b, w_out, norm_ffn, w_router, b_router, w_gate, b_gate, w_up, b_up, w_down, b_down):
    d = x_prompt.shape[-1]
    w_a = w_br_a.shape[1]
    w_b = w_br_b.shape[1]
    n_experts = w_router.shape[-1]
    groups = [x_prompt, x_sample]

    row0s, seqs = [], []
    m_real = 0
    for x in groups:
        row0s.append(m_real)
        m_real += x.shape[0] * x.shape[1]
    n_seq = sum(x.shape[0] for x in groups)
    meta_row0 = m_real
    m_tok = m_real + n_seq * N_META
    m_pad = _round_up(m_tok + 1, ROW_TILE)
    h_all = jnp.concatenate(
        [x.reshape(-1, d) for x in groups]
        + [jnp.tile(meta_tokens.astype(F32), (n_seq, 1)), jnp.zeros((m_pad - m_tok, d), F32)], axis=0)

    hn = _rmsnorm_rows(h_all, norm_mix[0], BF16)
    scale_a, scale_b = HD ** -0.5 * LOG2E, HD ** -0.5 * LOG2E
    ones = lambda n: jnp.ones((n,), F32)
    gain = jnp.concatenate([
        jnp.tile(qk_norm_a_q[0].astype(F32) * scale_a, w_a // HD), jnp.tile(qk_norm_a_k[0].astype(F32), w_a // HD),
        ones(w_a),
        jnp.tile(qk_norm_b_q[0].astype(F32) * scale_b, w_b // HD), jnp.tile(qk_norm_b_k[0].astype(F32), w_b // HD),
        ones(w_b + 2 * d)]).reshape(1, -1)
    proj = _inproj(hn, w_in[0].astype(BF16), gain, w_a, w_b)

    lam_vecs = [v[0].reshape(1, HD).astype(F32) for v in (lambda_q1, lambda_k1, lambda_q2, lambda_k2)]
    subln = subln_a[0].reshape(1, 2 * HD).astype(F32)
    na_bias = _na_bias_tiles(na_rpb[0].astype(F32) * LOG2E)
    rel_bias2 = rel_bias.astype(F32) * LOG2E
    oa = jnp.zeros((m_pad, w_a), BF16)
    ob = jnp.zeros((m_pad, w_b), BF16)
    tile_cache = {}
    seq0 = 0
    for x, row0 in zip(groups, row0s):
        nb, s = x.shape[0], x.shape[1]
        mrow = meta_row0 + seq0 * N_META
        tq = min(ATTN_TQ, s)
        common = dict(w_a=w_a, n_seq=nb, seq_len=s, kv_row0=row0, meta_row0=mrow, tk=min(ATTN_TK, s))
        oa = _diff_attn(proj, oa, rel_bias2, lam_vecs, subln, tile_cache, q_row0=row0,
                        q_stride=s, tq=tq, nq=s // tq, qpos0=N_META, **common)
        oa = _diff_attn(proj, oa, rel_bias2, lam_vecs, subln, tile_cache, q_row0=mrow,
                        q_stride=N_META, tq=N_META, nq=1, qpos0=0, **common)
        ob = _na_attn(proj, ob, na_bias, w_a=w_a, w_b=w_b, n_seq=nb, seq_len=s, row0=row0, meta_row0=mrow)
        seq0 += nb
    ob = _na_meta_attn(proj, ob, w_a=w_a, w_b=w_b, n_seq=n_seq, meta_row0=meta_row0)

    merged = _merge(oa, ob, w_br_a[0].astype(BF16), w_br_b[0].astype(BF16), proj, 3 * w_a + 3 * w_b)
    h1 = _outproj(merged, w_out[0].astype(BF16), h_all)

    hf, top_i, top_w = _router(h1, norm_ffn[0], w_router[0], b_router[0])
    pos, block_e, n_used, n_valid, nb_max = _dispatch_plan(top_i[:m_tok, :TOP_K], n_experts, MOE_TILE)
    slot_rows = jnp.full((nb_max * MOE_TILE,), m_tok, I32).at[pos].set(
        jnp.repeat(jnp.arange(m_tok, dtype=I32), TOP_K))
    per_blk = MOE_TILE // MOE_SUB_TILE
    piece_valid = jnp.clip(n_valid[:, None] - jnp.arange(per_blk, dtype=I32)[None, :] * MOE_SUB_TILE,
                           0, MOE_SUB_TILE).reshape(-1)
    xs = _gather_rows(hf, slot_rows, n_used * per_blk, piece_valid, MOE_SUB_TILE)
    ys = _moe_experts(xs, block_e, n_used, n_valid, w_gate[0].astype(F32), w_up[0].astype(F32),
                      w_down[0].astype(F32), b_gate[0].astype(F32), b_up[0].astype(F32),
                      b_down[0].astype(F32), MOE_TILE)
    pos2 = pos.reshape(m_tok, TOP_K)
    outs = []
    for x, row0 in zip(groups, row0s):
        n_rows = x.shape[0] * x.shape[1]
        outs.append(_combine(h1, top_w, pos2, ys, row0, n_rows).reshape(x.shape))
    return tuple(outs)
```

```python
import functools
import math

import numpy as np
import jax
import jax.numpy as jnp
from jax import lax
from jax.experimental import pallas as pl
from jax.experimental.pallas import tpu as pltpu

F32 = jnp.float32
BF16 = jnp.bfloat16
U32 = jnp.uint32
I32 = jnp.int32

N_META = 16
GRID_W = 64
HD = 128
REL_BUCKETS = 32
REL_MAX_DIST = 128
NA_MAX_ROWS = 8
NA_COLS = 16
TOP_K = 4
SWIGLU_LIMIT = 7.0
SWIGLU_ALPHA = 1.702
EPS = 1e-6
LAMBDA_INIT = 0.8 - 0.6 * math.exp(-0.3 * 0)
NEG = -1e30
LOG2E = math.log2(math.e)

LANES = 128
VMEM_LIMIT_MB = 56
ROW_TILE = 1024
COL_TILE = 1024
NORM_TILE = 256
NA_ROWS = 8
NA_BLOCK = NA_ROWS * GRID_W
NA_WIN_ROWS = NA_ROWS + NA_MAX_ROWS
NA_WIN = NA_WIN_ROWS * GRID_W
NA_KBLOCK = NA_WIN // 4
NA_HEADS = 2
MOE_TILE = 1536
MOE_SUB_TILE = 512
MOE_FF_TILE = 256
MOE_DOWN_TILE = 512
ATTN_TQ = 1024
ATTN_TK = 2048
COMBINE_TILE = 128
DMA_UNROLL = 8


def _params(semantics, vmem_mb=VMEM_LIMIT_MB):
    return pltpu.CompilerParams(dimension_semantics=semantics,
                                vmem_limit_bytes=vmem_mb * 2**20)


def _rmsnorm_kernel(x_ref, g_ref, o_ref):
    x = x_ref[...].astype(F32)
    ms = jnp.mean(x * x, axis=-1, keepdims=True)
    o_ref[...] = (x * lax.rsqrt(ms + EPS) * g_ref[...]).astype(o_ref.dtype)


def _rmsnorm_rows(x, g, out_dtype):
    m, d = x.shape
    tm = min(NORM_TILE, m)
    return pl.pallas_call(
        _rmsnorm_kernel,
        grid=(m // tm,),
        in_specs=[pl.BlockSpec((tm, d), lambda i: (i, 0)),
                  pl.BlockSpec((1, d), lambda i: (0, 0))],
        out_specs=pl.BlockSpec((tm, d), lambda i: (i, 0)),
        out_shape=jax.ShapeDtypeStruct((m, d), out_dtype),
        compiler_params=_params(("parallel",)),
        name="rmsnorm_rows",
    )(x, g.reshape(1, d).astype(F32))


def _inproj_kernel(x_ref, w_ref, gain_ref, o_ref, *, norm_ranges, gate_start):
    j = pl.program_id(0)
    acc = jnp.dot(x_ref[...], w_ref[...], preferred_element_type=F32)
    tn = acc.shape[1]

    is_norm = False
    for lo, hi in norm_ranges:
        is_norm = jnp.logical_or(is_norm, jnp.logical_and(j >= lo, j < hi))
    is_gate = j >= gate_start

    @pl.when(is_norm)
    def _():
        for c in range(tn // HD):
            blk = acc[:, c * HD:(c + 1) * HD]
            ms = jnp.mean(blk * blk, axis=-1, keepdims=True)
            o_ref[:, c * HD:(c + 1) * HD] = (
                blk * lax.rsqrt(ms + EPS) * gain_ref[:, c * HD:(c + 1) * HD]).astype(o_ref.dtype)

    @pl.when(is_gate)
    def _():
        o_ref[...] = (1.0 / (1.0 + jnp.exp(-acc))).astype(o_ref.dtype)

    @pl.when(jnp.logical_not(jnp.logical_or(is_norm, is_gate)))
    def _():
        o_ref[...] = acc.astype(o_ref.dtype)


def _inproj(hn, w, gain, w_a, w_b):
    m, d = hn.shape
    n = w.shape[1]
    tm = min(ROW_TILE, m)
    tn = min(COL_TILE, w_a, w_b)
    assert w_a % tn == 0 and w_b % tn == 0 and n % tn == 0 and m % tm == 0
    norm_ranges = ((0, 2 * w_a // tn), (3 * w_a // tn, (3 * w_a + 2 * w_b) // tn))
    gate_start = (3 * w_a + 3 * w_b) // tn
    kern = functools.partial(_inproj_kernel, norm_ranges=norm_ranges, gate_start=gate_start)
    return pl.pallas_call(
        kern,
        grid=(n // tn, m // tm),
        in_specs=[pl.BlockSpec((tm, d), lambda j, i: (i, 0)),
                  pl.BlockSpec((d, tn), lambda j, i: (0, j)),
                  pl.BlockSpec((1, tn), lambda j, i: (0, j))],
        out_specs=pl.BlockSpec((tm, tn), lambda j, i: (i, j)),
        out_shape=jax.ShapeDtypeStruct((m, n), BF16),
        compiler_params=_params(("parallel", "parallel")),
        name="inproj",
    )(hn, w, gain)


def _t5_bucket(rel):
    half = REL_BUCKETS // 2
    max_exact = half // 2
    sign = (rel > 0).astype(I32) * half
    n = jnp.abs(rel)
    nf = jnp.maximum(n, 1).astype(F32)
    large = max_exact + (jnp.log(nf / max_exact) / math.log(REL_MAX_DIST / max_exact)
                         * (half - max_exact)).astype(I32)
    large = jnp.minimum(large, half - 1)
    return sign + jnp.where(n < max_exact, n, large)


def _bias_tile_plan(qpos0, nq, tq, nk, tk):
    keys, ids = [], np.zeros((nq, nk), np.int32)
    for i in range(nq):
        for j in range(nk):
            d = (N_META + j * tk) - (qpos0 + i * tq)
            if d - (tq - 1) >= REL_MAX_DIST:
                key = ("far", REL_MAX_DIST)
            elif d + (tk - 1) <= -REL_MAX_DIST:
                key = ("far", -REL_MAX_DIST)
            else:
                key = ("near", d)
            if key not in keys:
                keys.append(key)
            ids[i, j] = keys.index(key)
    return tuple(keys), ids.reshape(-1)


def _bias_tiles(rel_bias, keys, tq, tk):
    h = rel_bias.shape[1]
    blocked = tq % LANES == 0 and tk % LANES == 0 and all(v % LANES == 0 for _, v in keys)
    if not blocked:
        ramp = np.arange(tk)[None, :] - np.arange(tq)[:, None]
        rel = np.stack([np.full((tq, tk), v) if kind == "far" else v + ramp for kind, v in keys])
        b = rel_bias[_t5_bucket(jnp.asarray(rel.astype(np.int32)))]
        return jnp.transpose(b, (3, 0, 1, 2)).astype(F32)

    def const(v):
        return rel_bias[_t5_bucket(jnp.full((1,), v, I32))].reshape(h, 1, 1, 1, 1)

    r = np.arange(LANES)
    rel3 = np.stack([LANES * dl + r[None, :] - r[:, None] for dl in (-1, 0, 1)]).astype(np.int32)
    f3 = jnp.transpose(rel_bias[_t5_bucket(jnp.asarray(rel3))], (3, 0, 1, 2))
    na, nc = tq // LANES, tk // LANES
    tiles = []
    for kind, v in keys:
        if kind == "far":
            tiles.append(jnp.broadcast_to(const(v).reshape(h, 1, 1), (h, tq, tk)))
            continue
        delta = v // LANES + np.arange(nc)[None, :] - np.arange(na)[:, None]
        t5 = jnp.where(jnp.asarray(delta < 0)[None, :, None, :, None], const(-REL_MAX_DIST), const(REL_MAX_DIST))
        for dl in (-1, 0, 1):
            t5 = jnp.where(jnp.asarray(delta == dl)[None, :, None, :, None], f3[:, dl + 1][:, None, :, None, :], t5)
        tiles.append(t5.reshape(h, tq, tk))
    return jnp.stack(tiles, axis=1).astype(F32)


def _meta_bias_tiles(rel_bias, qpos0, nq, tq):
    n = 1
    while n < nq and qpos0 + (n - 1) * tq - (N_META - 1) < REL_MAX_DIST:
        n += 1
    qpos = qpos0 + np.arange(n * tq).reshape(n, tq)
    rel = np.arange(N_META)[None, None, :] - qpos[:, :, None]
    b = jnp.transpose(rel_bias[_t5_bucket(jnp.asarray(rel.astype(np.int32)))], (3, 0, 1, 2)).astype(F32)
    pad = jnp.full(b.shape[:3] + (LANES - N_META,), NEG, F32)
    return jnp.concatenate([b, pad], axis=-1), n


def _softmax_step(s, m_ref, l_ref, a_ref, v, first):
    m_cur = jnp.max(s, axis=-1, keepdims=True)
    if first:
        m_new = m_cur
        p = jnp.exp2(s - m_new)
        l_ref[...] = jnp.sum(p, axis=-1, keepdims=True)
        a_ref[...] = jnp.dot(p.astype(v.dtype), v, preferred_element_type=F32)
    else:
        m_prev = m_ref[...]
        m_new = jnp.maximum(m_prev, m_cur)
        alpha = jnp.exp2(m_prev - m_new)
        p = jnp.exp2(s - m_new)
        l_ref[...] = alpha * l_ref[...] + jnp.sum(p, axis=-1, keepdims=True)
        a_ref[...] = alpha * a_ref[...] + jnp.dot(p.astype(v.dtype), v, preferred_element_type=F32)
    m_ref[...] = m_new


def _qk(q, k):
    return lax.dot_general(q, k, (((1,), (1,)), ((), ())), preferred_element_type=F32)


def _pad_meta(x):
    return jnp.concatenate([x, jnp.zeros((LANES - N_META, x.shape[1]), x.dtype)], axis=0)


def _diff_attn_kernel(tid_ref, q_ref, k_ref, v_ref, km_ref, vm_ref, bias_ref, biasm_ref,
                      lq1_ref, lk1_ref, lq2_ref, lk2_ref, sub_ref, prev_ref, o_ref,
                      m1, l1, a1, m2, l2, a2, *, nk):
    del tid_ref, prev_ref
    j = pl.program_id(3)
    q = q_ref[...]
    q1, q2 = q[:, :HD], q[:, HD:]

    @pl.when(j == 0)
    def _():
        km = _pad_meta(km_ref[...])
        vm = _pad_meta(vm_ref[...])
        bm = biasm_ref[0, 0]
        _softmax_step(_qk(q1, km[:, :HD]) + bm, m1, l1, a1, vm, True)
        _softmax_step(_qk(q2, km[:, HD:]) + bm, m2, l2, a2, vm, True)

    k = k_ref[...]
    v = v_ref[...]
    b = bias_ref[0, 0]
    _softmax_step(_qk(q1, k[:, :HD]) + b, m1, l1, a1, v, False)
    _softmax_step(_qk(q2, k[:, HD:]) + b, m2, l2, a2, v, False)

    @pl.when(j == nk - 1)
    def _():
        lam = (jnp.exp(jnp.sum(lq1_ref[...] * lk1_ref[...], axis=-1, keepdims=True))
               - jnp.exp(jnp.sum(lq2_ref[...] * lk2_ref[...], axis=-1, keepdims=True))
               + LAMBDA_INIT)
        o = a1[...] / l1[...] - lam * (a2[...] / l2[...])
        ms = jnp.mean(o * o, axis=-1, keepdims=True)
        o_ref[...] = (o * lax.rsqrt(ms + EPS) * sub_ref[...] * (1.0 - LAMBDA_INIT)).astype(o_ref.dtype)


def _diff_attn(proj, oa, rel_bias, lam_vecs, subln, tile_cache, *, w_a, n_seq, seq_len, kv_row0,
               q_row0, q_stride, tq, nq, qpos0, meta_row0, tk):
    h_a = w_a // (2 * HD)
    nk = seq_len // tk
    assert seq_len % tk == 0 and kv_row0 % tk == 0 and q_row0 % tq == 0 and q_stride % tq == 0
    assert meta_row0 % N_META == 0
    keys, ids = _bias_tile_plan(qpos0, nq, tq, nk, tk)
    if (keys, tq, tk) not in tile_cache:
        tile_cache[(keys, tq, tk)] = _bias_tiles(rel_bias, keys, tq, tk)
    tiles = tile_cache[(keys, tq, tk)]
    biasm, n_bm = _meta_bias_tiles(rel_bias, qpos0, nq, tq)
    qb0, qbs = q_row0 // tq, q_stride // tq
    kb0, kbs = kv_row0 // tk, seq_len // tk
    mb0 = meta_row0 // N_META
    kcol, vcol = w_a // (2 * HD), 2 * w_a // (2 * HD)
    dv = 2 * HD

    def qmap(b, h, i, j, t):
        return (qb0 + b * qbs + i, h)

    vec = pl.BlockSpec((1, HD), lambda b, h, i, j, t: (0, 0))
    grid_spec = pltpu.PrefetchScalarGridSpec(
        num_scalar_prefetch=1,
        grid=(n_seq, h_a, nq, nk),
        in_specs=[
            pl.BlockSpec((tq, dv), qmap),
            pl.BlockSpec((tk, dv), lambda b, h, i, j, t: (kb0 + b * kbs + j, kcol + h)),
            pl.BlockSpec((tk, dv), lambda b, h, i, j, t: (kb0 + b * kbs + j, vcol + h)),
            pl.BlockSpec((N_META, dv), lambda b, h, i, j, t: (mb0 + b, kcol + h)),
            pl.BlockSpec((N_META, dv), lambda b, h, i, j, t: (mb0 + b, vcol + h)),
            pl.BlockSpec((1, 1, tq, tk), lambda b, h, i, j, t: (h, t[i * nk + j], 0, 0)),
            pl.BlockSpec((1, 1, tq, LANES), lambda b, h, i, j, t: (h, jnp.minimum(i, n_bm - 1), 0, 0)),
            vec, vec, vec, vec,
            pl.BlockSpec((1, dv), lambda b, h, i, j, t: (0, 0)),
            pl.BlockSpec(memory_space=pl.ANY),
        ],
        out_specs=pl.BlockSpec((tq, dv), qmap),
        scratch_shapes=[pltpu.VMEM((tq, 1), F32), pltpu.VMEM((tq, 1), F32), pltpu.VMEM((tq, dv), F32),
                        pltpu.VMEM((tq, 1), F32), pltpu.VMEM((tq, 1), F32), pltpu.VMEM((tq, dv), F32)],
    )
    return pl.pallas_call(
        functools.partial(_diff_attn_kernel, nk=nk),
        grid_spec=grid_spec,
        out_shape=jax.ShapeDtypeStruct(oa.shape, oa.dtype),
        input_output_aliases={13: 0},
        compiler_params=_params(("parallel", "parallel", "parallel", "arbitrary")),
        name=f"diff_attn_q{tq}",
    )(jnp.asarray(ids), proj, proj, proj, proj, proj, tiles, biasm, *lam_vecs, subln, oa)


def _na_plan():
    rows_c = 3 * NA_ROWS
    a = np.arange(NA_ROWS)
    kl = np.arange(NA_WIN_ROWS)
    ridx = np.zeros((3, NA_ROWS, NA_WIN_ROWS), np.int32)
    rvalid = np.zeros((3, NA_ROWS, NA_WIN_ROWS), bool)
    for v in range(3):
        r = v * NA_ROWS + a
        rs = np.clip(r - NA_MAX_ROWS // 2, 0, rows_c - NA_MAX_ROWS)
        kr = v * NA_ROWS - NA_MAX_ROWS // 2 + kl
        off = kr[None, :] - r[:, None]
        ok = (kr[None, :] >= rs[:, None]) & (kr[None, :] < rs[:, None] + NA_MAX_ROWS)
        ok &= (kr[None, :] >= 0) & (kr[None, :] < rows_c)
        ridx[v] = np.clip(off + NA_MAX_ROWS - 1, 0, 2 * NA_MAX_ROWS - 2)
        rvalid[v] = ok
    c = np.arange(GRID_W)
    cs = np.clip(c - NA_COLS // 2, 0, GRID_W - NA_COLS)
    coff = c[None, :] - c[:, None]
    cvalid = (c[None, :] >= cs[:, None]) & (c[None, :] < cs[:, None] + NA_COLS)
    cidx = np.clip(coff + NA_COLS - 1, 0, 2 * NA_COLS - 2).astype(np.int32)
    return ridx, rvalid, cidx, cvalid


def _na_bias_tiles(rpb):
    ridx, rvalid, cidx, cvalid = _na_plan()
    h = rpb.shape[0]
    t1 = rpb[:, :, cidx]
    t2 = t1[:, ridx]
    valid = rvalid[:, :, :, None, None] & cvalid[None, None, None]
    t2 = jnp.where(jnp.asarray(valid)[None], t2, NEG)
    t2 = jnp.transpose(t2, (0, 1, 2, 4, 3, 5))
    return t2.reshape(h, 3, NA_BLOCK, NA_WIN).astype(F32)


def _na_kernel(q_ref, k0_ref, k1_ref, k2_ref, k3_ref, v0_ref, v1_ref, v2_ref, v3_ref, km_ref, vm_ref,
               bias_ref, prev_ref, o_ref):
    del prev_ref
    k_all = jnp.concatenate([k0_ref[...], k1_ref[...], k2_ref[...], k3_ref[...]], axis=0)
    v_all = jnp.concatenate([v0_ref[...], v1_ref[...], v2_ref[...], v3_ref[...]], axis=0)
    km_all = _pad_meta(km_ref[...])
    vm_all = _pad_meta(vm_ref[...])
    lane = lax.broadcasted_iota(I32, (q_ref.shape[0], LANES), 1)
    for g in range(NA_HEADS):
        cols = slice(g * HD, (g + 1) * HD)
        q, k, v, km, vm = q_ref[:, cols], k_all[:, cols], v_all[:, cols], km_all[:, cols], vm_all[:, cols]
        s = _qk(q, k) + bias_ref[g, 0]
        sm = jnp.where(lane < N_META, _qk(q, km), NEG)
        m = jnp.maximum(jnp.max(s, axis=-1, keepdims=True), jnp.max(sm, axis=-1, keepdims=True))
        p = jnp.exp2(s - m)
        pm = jnp.exp2(sm - m)
        l = jnp.sum(p, axis=-1, keepdims=True) + jnp.sum(pm, axis=-1, keepdims=True)
        o = (jnp.dot(p.astype(v.dtype), v, preferred_element_type=F32)
             + jnp.dot(pm.astype(vm.dtype), vm, preferred_element_type=F32))
        o_ref[:, cols] = (o / l).astype(o_ref.dtype)


def _na_attn(proj, ob, bias, *, w_a, w_b, n_seq, seq_len, row0, meta_row0):
    gw = NA_HEADS * HD
    nblk = seq_len // NA_BLOCK
    assert seq_len % NA_BLOCK == 0 and nblk >= 2 and row0 % NA_BLOCK == 0
    assert w_b % gw == 0 and (3 * w_a) % gw == 0
    rb0 = row0 // NA_BLOCK
    mb0 = meta_row0 // N_META
    qcol, kcol, vcol = 3 * w_a // gw, (3 * w_a + w_b) // gw, (3 * w_a + 2 * w_b) // gw

    per_q = NA_BLOCK // NA_KBLOCK
    n_kb = nblk * per_q

    def kblk(col, t):
        def index(h, b, i):
            return (rb0 * per_q + b * n_kb + jnp.clip(i * per_q - 1 + t, 0, n_kb - 1), col + h)
        return pl.BlockSpec((NA_KBLOCK, gw), index)

    def variant(i):
        return jnp.where(i == 0, 0, jnp.where(i == nblk - 1, 2, 1))

    n_win = NA_WIN // NA_KBLOCK
    return pl.pallas_call(
        _na_kernel,
        grid=(w_b // gw, n_seq, nblk),
        in_specs=[pl.BlockSpec((NA_BLOCK, gw), lambda h, b, i: (rb0 + b * nblk + i, qcol + h))]
                 + [kblk(kcol, t) for t in range(n_win)] + [kblk(vcol, t) for t in range(n_win)]
                 + [pl.BlockSpec((N_META, gw), lambda h, b, i: (mb0 + b, kcol + h)),
                    pl.BlockSpec((N_META, gw), lambda h, b, i: (mb0 + b, vcol + h)),
                    pl.BlockSpec((NA_HEADS, 1, NA_BLOCK, NA_WIN), lambda h, b, i: (h, variant(i), 0, 0)),
                    pl.BlockSpec(memory_space=pl.ANY)],
        out_specs=pl.BlockSpec((NA_BLOCK, gw), lambda h, b, i: (rb0 + b * nblk + i, h)),
        out_shape=jax.ShapeDtypeStruct(ob.shape, ob.dtype),
        input_output_aliases={2 * n_win + 4: 0},
        compiler_params=_params(("parallel", "parallel", "parallel")),
        name="na_attn",
    )(*([proj] * (2 * n_win + 3)), bias, ob)


def _na_meta_kernel(q_ref, km_ref, vm_ref, prev_ref, o_ref):
    del prev_ref
    q = jnp.concatenate([q_ref[...], jnp.zeros((LANES - N_META, HD), q_ref.dtype)], axis=0)
    km = _pad_meta(km_ref[...])
    vm = _pad_meta(vm_ref[...])
    lane = lax.broadcasted_iota(I32, (LANES, LANES), 1)
    s = jnp.where(lane < N_META, _qk(q, km), NEG)
    m = jnp.max(s, axis=-1, keepdims=True)
    p = jnp.exp2(s - m)
    o = jnp.dot(p.astype(vm.dtype), vm, preferred_element_type=F32) / jnp.sum(p, axis=-1, keepdims=True)
    o_ref[...] = o[:N_META].astype(o_ref.dtype)


def _na_meta_attn(proj, ob, *, w_a, w_b, n_seq, meta_row0):
    h_b = w_b // HD
    mb0 = meta_row0 // N_META
    qcol, kcol, vcol = 3 * w_a // HD, (3 * w_a + w_b) // HD, (3 * w_a + 2 * w_b) // HD
    return pl.pallas_call(
        _na_meta_kernel,
        grid=(n_seq, h_b),
        in_specs=[pl.BlockSpec((N_META, HD), lambda s, h: (mb0 + s, qcol + h)),
                  pl.BlockSpec((N_META, HD), lambda s, h: (mb0 + s, kcol + h)),
                  pl.BlockSpec((N_META, HD), lambda s, h: (mb0 + s, vcol + h)),
                  pl.BlockSpec(memory_space=pl.ANY)],
        out_specs=pl.BlockSpec((N_META, HD), lambda s, h: (mb0 + s, h)),
        out_shape=jax.ShapeDtypeStruct(ob.shape, ob.dtype),
        input_output_aliases={3: 0},
        compiler_params=_params(("parallel", "parallel")),
        name="na_meta_attn",
    )(proj, proj, proj, ob)


def _merge_kernel(oa_ref, ob_ref, wa_ref, wb_ref, ga_ref, gb_ref, o_ref):
    a = jnp.dot(oa_ref[...], wa_ref[...], preferred_element_type=F32)
    b = jnp.dot(ob_ref[...], wb_ref[...], preferred_element_type=F32)
    o_ref[...] = (ga_ref[...].astype(F32) * a + gb_ref[...].astype(F32) * b).astype(o_ref.dtype)


def _merge(oa, ob, wa, wb, proj, gate_col0):
    m, w_a = oa.shape
    w_b = ob.shape[1]
    d = wa.shape[1]
    tm = min(ROW_TILE, m)
    tn = min(COL_TILE, d)
    ga0 = gate_col0 // tn
    gb0 = (gate_col0 + d) // tn
    return pl.pallas_call(
        _merge_kernel,
        grid=(d // tn, m // tm),
        in_specs=[pl.BlockSpec((tm, w_a), lambda j, i: (i, 0)),
                  pl.BlockSpec((tm, w_b), lambda j, i: (i, 0)),
                  pl.BlockSpec((w_a, tn), lambda j, i: (0, j)),
                  pl.BlockSpec((w_b, tn), lambda j, i: (0, j)),
                  pl.BlockSpec((tm, tn), lambda j, i: (i, ga0 + j)),
                  pl.BlockSpec((tm, tn), lambda j, i: (i, gb0 + j))],
        out_specs=pl.BlockSpec((tm, tn), lambda j, i: (i, j)),
        out_shape=jax.ShapeDtypeStruct((m, d), BF16),
        compiler_params=_params(("parallel", "parallel")),
        name="merge",
    )(oa, ob, wa, wb, proj, proj)


def _outproj_kernel(x_ref, w_ref, h_ref, o_ref):
    o_ref[...] = h_ref[...] + jnp.dot(x_ref[...], w_ref[...], preferred_element_type=F32)


def _outproj(merged, w, h):
    m, d = merged.shape
    n = w.shape[1]
    tm = min(ROW_TILE, m)
    tn = min(COL_TILE, n)
    return pl.pallas_call(
        _outproj_kernel,
        grid=(n // tn, m // tm),
        in_specs=[pl.BlockSpec((tm, d), lambda j, i: (i, 0)),
                  pl.BlockSpec((d, tn), lambda j, i: (0, j)),
                  pl.BlockSpec((tm, tn), lambda j, i: (i, j))],
        out_specs=pl.BlockSpec((tm, tn), lambda j, i: (i, j)),
        out_shape=jax.ShapeDtypeStruct((m, n), F32),
        compiler_params=_params(("parallel", "parallel")),
        name="outproj",
    )(merged, w, h)


def _pack_bf16_pairs(x):
    half = x.shape[1] // 2
    lo = lax.bitcast_convert_type(x[:, :half].astype(jnp.bfloat16).astype(F32), U32)
    hi = lax.bitcast_convert_type(x[:, half:].astype(jnp.bfloat16).astype(F32), U32)
    return (lo >> 16) | (hi & jnp.uint32(0xFFFF0000))


def _unpack_bf16_pairs(u):
    lo = lax.bitcast_convert_type(u << 16, F32).astype(BF16)
    hi = lax.bitcast_convert_type(u & jnp.uint32(0xFFFF0000), F32).astype(BF16)
    return jnp.concatenate([lo, hi], axis=1)


def _router_kernel(h_ref, g_ref, wr_ref, br_ref, hf_ref, ti_ref, tw_ref):
    x = h_ref[...]
    ms = jnp.mean(x * x, axis=-1, keepdims=True)
    hf = x * lax.rsqrt(ms + EPS) * g_ref[...]
    hf_ref[...] = _pack_bf16_pairs(hf)
    logits = jnp.dot(hf, wr_ref[...], preferred_element_type=F32,
                     precision=lax.Precision.HIGHEST) + br_ref[...]
    lane = lax.broadcasted_iota(I32, logits.shape, 1).astype(F32)
    vals, idxs = [], []
    for _ in range(TOP_K):
        mx = jnp.max(logits, axis=-1, keepdims=True)
        ix = jnp.min(jnp.where(logits == mx, lane, float(LANES)), axis=-1, keepdims=True)
        vals.append(mx)
        idxs.append(ix)
        logits = jnp.where(lane == ix, -jnp.inf, logits)
    es = [jnp.exp(v - vals[0]) for v in vals]
    tot = es[0]
    for e in es[1:]:
        tot = tot + e
    ti = jnp.zeros(lane.shape, F32)
    tw = jnp.zeros(lane.shape, F32)
    for k in range(TOP_K):
        ti = jnp.where(lane == k, idxs[k], ti)
        tw = jnp.where(lane == k, es[k] / tot, tw)
    ti_ref[...] = ti.astype(I32)
    tw_ref[...] = tw


def _router(h1, g, w_router, b_router):
    m, d = h1.shape
    e = w_router.shape[1]
    assert e <= LANES
    tm = min(NORM_TILE, m)
    wr = jnp.zeros((d, LANES), F32).at[:, :e].set(w_router.astype(F32))
    br = jnp.full((1, LANES), NEG, F32).at[0, :e].set(b_router.astype(F32))
    return pl.pallas_call(
        _router_kernel,
        grid=(m // tm,),
        in_specs=[pl.BlockSpec((tm, d), lambda i: (i, 0)),
                  pl.BlockSpec((1, d), lambda i: (0, 0)),
                  pl.BlockSpec((d, LANES), lambda i: (0, 0)),
                  pl.BlockSpec((1, LANES), lambda i: (0, 0))],
        out_specs=[pl.BlockSpec((tm, d // 2), lambda i: (i, 0)),
                   pl.BlockSpec((tm, LANES), lambda i: (i, 0)),
                   pl.BlockSpec((tm, LANES), lambda i: (i, 0))],
        out_shape=[jax.ShapeDtypeStruct((m, d // 2), U32),
                   jax.ShapeDtypeStruct((m, LANES), I32),
                   jax.ShapeDtypeStruct((m, LANES), F32)],
        compiler_params=_params(("parallel",)),
        name="router",
    )(h1, g.reshape(1, d).astype(F32), wr, br)


def _row_copy(src_ref, src_row, dst_ref, dst_row, sem):
    return pltpu.make_async_copy(src_ref.at[pl.ds(src_row, 1)], dst_ref.at[pl.ds(dst_row, 1)], sem)


def _gather_kernel(nu_ref, nv_ref, idx_ref, idx_next_ref, src_ref, o_ref, buf, sems, *, n_blocks):
    b = pl.program_id(0)
    nu = nu_ref[0]

    def n_chunks(blk):
        return lax.div(nv_ref[blk] + (DMA_UNROLL - 1), DMA_UNROLL)

    def issue(rows_ref, slot, blk):
        def body(c, carry):
            for u in range(DMA_UNROLL):
                r = c * DMA_UNROLL + u
                _row_copy(src_ref, rows_ref[0, r], buf.at[slot], r, sems.at[slot]).start()
            return carry
        lax.fori_loop(0, n_chunks(blk), body, 0)

    def drain(slot, blk):
        def body(c, carry):
            for u in range(DMA_UNROLL):
                _row_copy(src_ref, 0, buf.at[slot], c * DMA_UNROLL + u, sems.at[slot]).wait()
            return carry
        lax.fori_loop(0, n_chunks(blk), body, 0)

    @pl.when(b == 0)
    def _():
        buf[...] = jnp.zeros(buf.shape, buf.dtype)
        issue(idx_ref, 0, 0)

    for slot in (0, 1):
        @pl.when(jnp.logical_and(b % 2 == slot, b < nu))
        def _():
            @pl.when(b + 1 < nu)
            def _():
                issue(idx_next_ref, 1 - slot, jnp.minimum(b + 1, n_blocks - 1))
            drain(slot, b)
            o_ref[...] = _unpack_bf16_pairs(buf[slot])

    @pl.when(b >= nu)
    def _():
        o_ref[...] = jnp.zeros(o_ref.shape, o_ref.dtype)


def _gather_rows(src, slot_rows, n_used, n_valid, tile):
    n_slots = slot_rows.shape[0]
    half = src.shape[1]
    nb = n_slots // tile
    assert tile % DMA_UNROLL == 0
    slot_blocks = slot_rows.reshape(nb, 1, tile)
    return pl.pallas_call(
        functools.partial(_gather_kernel, n_blocks=nb),
        grid=(nb,),
        in_specs=[pl.BlockSpec(memory_space=pltpu.SMEM),
                  pl.BlockSpec(memory_space=pltpu.SMEM),
                  pl.BlockSpec((None, 1, tile), lambda b: (b, 0, 0), memory_space=pltpu.SMEM),
                  pl.BlockSpec((None, 1, tile), lambda b: (jnp.minimum(b + 1, nb - 1), 0, 0),
                               memory_space=pltpu.SMEM),
                  pl.BlockSpec(memory_space=pl.ANY)],
        out_specs=pl.BlockSpec((tile, 2 * half), lambda b: (b, 0)),
        out_shape=jax.ShapeDtypeStruct((n_slots, 2 * half), BF16),
        scratch_shapes=[pltpu.VMEM((2, tile, half), src.dtype), pltpu.SemaphoreType.DMA((2,))],
        compiler_params=_params(("arbitrary",)),
        name="moe_gather",
    )(n_used, n_valid, slot_blocks, slot_blocks, src)


def _swiglu(x, wg, wu, bg, bu):
    g = jnp.dot(x, wg, preferred_element_type=F32) + bg
    u = jnp.dot(x, wu, preferred_element_type=F32) + bu
    g = jnp.minimum(g, SWIGLU_LIMIT)
    u = jnp.clip(u, -SWIGLU_LIMIT, SWIGLU_LIMIT)
    return ((u + 1.0) * (g * (1.0 / (1.0 + jnp.exp(-SWIGLU_ALPHA * g))))).astype(BF16)


def _for_valid_rows(nv, tile, o_ref, compute):
    sub = min(MOE_SUB_TILE, tile)
    for rows in range(sub, tile + 1, sub):
        @pl.when(jnp.logical_and(nv > rows - sub, nv <= rows))
        def _():
            o_ref[:rows] = compute(rows)
            if rows < tile:
                o_ref[rows:] = jnp.zeros((tile - rows, o_ref.shape[1]), o_ref.dtype)


def _moe_up_kernel(be_ref, nu_ref, nv_ref, x_ref, wg_ref, wu_ref, bg_ref, bu_ref, o_ref):
    del be_ref
    b = pl.program_id(0)

    @pl.when(b < nu_ref[0])
    def _():
        wg = wg_ref[0].astype(BF16)
        wu = wu_ref[0].astype(BF16)
        _for_valid_rows(nv_ref[b], x_ref.shape[0], o_ref,
                        lambda rows: _swiglu(x_ref[:rows], wg, wu, bg_ref[0], bu_ref[0]))

    @pl.when(b >= nu_ref[0])
    def _():
        o_ref[...] = jnp.zeros(o_ref.shape, o_ref.dtype)


def _moe_down_kernel(be_ref, nu_ref, nv_ref, a_ref, wd_ref, bd_ref, o_ref):
    del be_ref
    b = pl.program_id(0)

    @pl.when(b < nu_ref[0])
    def _():
        wd = wd_ref[0].astype(BF16)
        _for_valid_rows(nv_ref[b], a_ref.shape[0], o_ref,
                        lambda rows: jnp.dot(a_ref[:rows], wd, preferred_element_type=F32) + bd_ref[0])

    @pl.when(b >= nu_ref[0])
    def _():
        o_ref[...] = jnp.zeros(o_ref.shape, o_ref.dtype)


def _moe_experts(xs, block_e, n_used, n_valid, wg, wu, wd, bg, bu, bd, tile):
    n_slots, d = xs.shape
    e, _, ff = wg.shape
    tf = min(MOE_FF_TILE, ff)
    tn = min(MOE_DOWN_TILE, d)
    nb, nf, nn = n_slots // tile, ff // tf, d // tn

    def bb(b, nu):
        return jnp.minimum(b, nu[0] - 1)

    def frozen(b, j, last, nu):
        return jnp.where(b < nu[0], j, last)

    up_spec = pltpu.PrefetchScalarGridSpec(
        num_scalar_prefetch=3,
        grid=(nb, nf),
        in_specs=[
            pl.BlockSpec((tile, d), lambda b, f, be, nu, nv: (bb(b, nu), 0)),
            pl.BlockSpec((1, d, tf), lambda b, f, be, nu, nv: (be[bb(b, nu)], 0, frozen(b, f, nf - 1, nu))),
            pl.BlockSpec((1, d, tf), lambda b, f, be, nu, nv: (be[bb(b, nu)], 0, frozen(b, f, nf - 1, nu))),
            pl.BlockSpec((1, 1, tf), lambda b, f, be, nu, nv: (be[bb(b, nu)], 0, frozen(b, f, nf - 1, nu))),
            pl.BlockSpec((1, 1, tf), lambda b, f, be, nu, nv: (be[bb(b, nu)], 0, frozen(b, f, nf - 1, nu))),
        ],
        out_specs=pl.BlockSpec((tile, tf), lambda b, f, be, nu, nv: (b, f)),
    )
    act = pl.pallas_call(
        _moe_up_kernel,
        grid_spec=up_spec,
        out_shape=jax.ShapeDtypeStruct((n_slots, ff), BF16),
        compiler_params=_params(("arbitrary", "arbitrary")),
        name="moe_up",
    )(block_e, n_used, n_valid, xs, wg, wu, bg.reshape(e, 1, ff), bu.reshape(e, 1, ff))

    down_spec = pltpu.PrefetchScalarGridSpec(
        num_scalar_prefetch=3,
        grid=(nb, nn),
        in_specs=[
            pl.BlockSpec((tile, ff), lambda b, n, be, nu, nv: (bb(b, nu), 0)),
            pl.BlockSpec((1, ff, tn), lambda b, n, be, nu, nv: (be[bb(b, nu)], 0, frozen(b, n, nn - 1, nu))),
            pl.BlockSpec((1, 1, tn), lambda b, n, be, nu, nv: (be[bb(b, nu)], 0, frozen(b, n, nn - 1, nu))),
        ],
        out_specs=pl.BlockSpec((tile, tn), lambda b, n, be, nu, nv: (b, n)),
    )
    return pl.pallas_call(
        _moe_down_kernel,
        grid_spec=down_spec,
        out_shape=jax.ShapeDtypeStruct((n_slots, d), F32),
        compiler_params=_params(("arbitrary", "arbitrary")),
        name="moe_down",
    )(block_e, n_used, n_valid, act, wd, bd.reshape(e, 1, d))


def _combine_kernel(pos_ref, pos_next_ref, w_ref, h_ref, ys_ref, o_ref, buf, sems, *, n):
    i = pl.program_id(0)
    rows = o_ref.shape[0]

    def issue(rows_ref, slot):
        def body(r, c):
            for k in range(TOP_K):
                _row_copy(ys_ref, rows_ref[0, TOP_K * r + k], buf.at[slot, k], r, sems.at[slot]).start()
            return c
        lax.fori_loop(0, rows, body, 0, unroll=DMA_UNROLL // 2)

    def drain(slot):
        def body(r, c):
            for k in range(TOP_K):
                _row_copy(ys_ref, 0, buf.at[slot, k], r, sems.at[slot]).wait()
            return c
        lax.fori_loop(0, rows, body, 0, unroll=DMA_UNROLL // 2)

    @pl.when(i == 0)
    def _():
        issue(pos_ref, 0)

    for slot in (0, 1):
        @pl.when(i % 2 == slot)
        def _():
            @pl.when(i + 1 < n)
            def _():
                issue(pos_next_ref, 1 - slot)
            drain(slot)
            w = w_ref[...]
            acc = h_ref[...]
            for k in range(TOP_K):
                acc = acc + w[:, k:k + 1] * buf[slot, k]
            o_ref[...] = acc


def _combine(h1, tw, pos, ys, row0, n_rows):
    d = h1.shape[1]
    tc = min(COMBINE_TILE, n_rows)
    assert n_rows % tc == 0 and row0 % tc == 0
    nb, b0 = n_rows // tc, row0 // tc
    pos_blocks = pos[row0:row0 + n_rows].reshape(nb, 1, tc * TOP_K)
    return pl.pallas_call(
        functools.partial(_combine_kernel, n=nb),
        grid=(nb,),
        in_specs=[pl.BlockSpec((None, 1, tc * TOP_K), lambda i: (i, 0, 0), memory_space=pltpu.SMEM),
                  pl.BlockSpec((None, 1, tc * TOP_K), lambda i: (jnp.minimum(i + 1, nb - 1), 0, 0),
                               memory_space=pltpu.SMEM),
                  pl.BlockSpec((tc, LANES), lambda i: (b0 + i, 0)),
                  pl.BlockSpec((tc, d), lambda i: (b0 + i, 0)),
                  pl.BlockSpec(memory_space=pl.ANY)],
        out_specs=pl.BlockSpec((tc, d), lambda i: (i, 0)),
        out_shape=jax.ShapeDtypeStruct((n_rows, d), F32),
        scratch_shapes=[pltpu.VMEM((2, TOP_K, tc, d), F32), pltpu.SemaphoreType.DMA((2,))],
        compiler_params=_params(("arbitrary",)),
        name="moe_combine",
    )(pos_blocks, pos_blocks, tw, h1, ys)


def _dispatch_plan(top_i, n_experts, tile):
    n_tok = top_i.shape[0]
    nk = n_tok * TOP_K
    flat_e = top_i.reshape(-1)
    onehot = (flat_e[:, None] == jnp.arange(n_experts, dtype=I32)[None, :]).astype(I32)
    rank = jnp.take_along_axis(jnp.cumsum(onehot, axis=0), flat_e[:, None], axis=1)[:, 0] - 1
    counts = jnp.sum(onehot, axis=0)
    nblk = (counts + tile - 1) // tile
    blk_end = jnp.cumsum(nblk)
    blk_start = blk_end - nblk
    pos = blk_start[flat_e] * tile + rank
    nb_max = -(-nk // tile) + n_experts
    block_e = jnp.clip(jnp.searchsorted(blk_end, jnp.arange(nb_max, dtype=I32), side="right"),
                       0, n_experts - 1).astype(I32)
    n_used = blk_end[-1:].astype(I32)
    blocks = jnp.arange(nb_max, dtype=I32)
    n_valid = jnp.clip(counts[block_e] - (blocks - blk_start[block_e]) * tile, 0, tile)
    n_valid = jnp.where(blocks < n_used[0], n_valid, 0).astype(I32)
    return pos.astype(I32), block_e, n_used, n_valid, nb_max


def _round_up(x, m):
    return -(-x // m) * m


def kernel(x_prompt, x_sample, meta_tokens, norm_mix, w_in, qk_norm_a_q, qk_norm_a_k, lambda_q1, lambda_k1, lambda_q2, lambda_k2, subln_a, rel_bias, qk_norm_b_q, qk_norm_b_k, na_rpb, w_br_a, w_br_b, w_out, norm_ffn, w_router, b_router, w_gate, b_gate, w_up, b_up, w_down, b_down):
    d = x_prompt.shape[-1]
    w_a = w_br_a.shape[1]
    w_b = w_br_b.shape[1]
    n_experts = w_router.shape[-1]
    groups = [x_prompt, x_sample]

    row0s, seqs = [], []
    m_real = 0
    for x in groups:
        row0s.append(m_real)
        m_real += x.shape[0] * x.shape[1]
    n_seq = sum(x.shape[0] for x in groups)
    meta_row0 = m_real
    m_tok = m_real + n_seq * N_META
    m_pad = _round_up(m_tok + 1, ROW_TILE)
    h_all = jnp.concatenate(
        [x.reshape(-1, d) for x in groups]
        + [jnp.tile(meta_tokens.astype(F32), (n_seq, 1)), jnp.zeros((m_pad - m_tok, d), F32)], axis=0)

    hn = _rmsnorm_rows(h_all, norm_mix[0], BF16)
    scale_a, scale_b = HD ** -0.5 * LOG2E, HD ** -0.5 * LOG2E
    ones = lambda n: jnp.ones((n,), F32)
    gain = jnp.concatenate([
        jnp.tile(qk_norm_a_q[0].astype(F32) * scale_a, w_a // HD), jnp.tile(qk_norm_a_k[0].astype(F32), w_a // HD),
        ones(w_a),
        jnp.tile(qk_norm_b_q[0].astype(F32) * scale_b, w_b // HD), jnp.tile(qk_norm_b_k[0].astype(F32), w_b // HD),
        ones(w_b + 2 * d)]).reshape(1, -1)
    proj = _inproj(hn, w_in[0].astype(BF16), gain, w_a, w_b)

    lam_vecs = [v[0].reshape(1, HD).astype(F32) for v in (lambda_q1, lambda_k1, lambda_q2, lambda_k2)]
    subln = subln_a[0].reshape(1, 2 * HD).astype(F32)
    na_bias = _na_bias_tiles(na_rpb[0].astype(F32) * LOG2E)
    rel_bias2 = rel_bias.astype(F32) * LOG2E
    oa = jnp.zeros((m_pad, w_a), BF16)
    ob = jnp.zeros((m_pad, w_b), BF16)
    tile_cache = {}
    seq0 = 0
    for x, row0 in zip(groups, row0s):
        nb, s = x.shape[0], x.shape[1]
        mrow = meta_row0 + seq0 * N_META
        tq = min(ATTN_TQ, s)
        common = dict(w_a=w_a, n_seq=nb, seq_len=s, kv_row0=row0, meta_row0=mrow, tk=min(ATTN_TK, s))
        oa = _diff_attn(proj, oa, rel_bias2, lam_vecs, subln, tile_cache, q_row0=row0,
                        q_stride=s, tq=tq, nq=s // tq, qpos0=N_META, **common)
        oa = _diff_attn(proj, oa, rel_bias2, lam_vecs, subln, tile_cache, q_row0=mrow,
                        q_stride=N_META, tq=N_META, nq=1, qpos0=0, **common)
        ob = _na_attn(proj, ob, na_bias, w_a=w_a, w_b=w_b, n_seq=nb, seq_len=s, row0=row0, meta_row0=mrow)
        seq0 += nb
    ob = _na_meta_attn(proj, ob, w_a=w_a, w_b=w_b, n_seq=n_seq, meta_row0=meta_row0)

    merged = _merge(oa, ob, w_br_a[0].astype(BF16), w_br_b[0].astype(BF16), proj, 3 * w_a + 3 * w_b)
    h1 = _outproj(merged, w_out[0].astype(BF16), h_all)

    hf, top_i, top_w = _router(h1, norm_ffn[0], w_router[0], b_router[0])
    pos, block_e, n_used, n_valid, nb_max = _dispatch_plan(top_i[:m_tok, :TOP_K], n_experts, MOE_TILE)
    slot_rows = jnp.full((nb_max * MOE_TILE,), m_tok, I32).at[pos].set(
        jnp.repeat(jnp.arange(m_tok, dtype=I32), TOP_K))
    per_blk = MOE_TILE // MOE_SUB_TILE
    piece_valid = jnp.clip(n_valid[:, None] - jnp.arange(per_blk, dtype=I32)[None, :] * MOE_SUB_TILE,
                           0, MOE_SUB_TILE).reshape(-1)
    xs = _gather_rows(hf, slot_rows, n_used * per_blk, piece_valid, MOE_SUB_TILE)
    ys = _moe_experts(xs, block_e, n_used, n_valid, w_gate[0].astype(F32), w_up[0].astype(F32),
                      w_down[0].astype(F32), b_gate[0].astype(F32), b_up[0].astype(F32),
                      b_down[0].astype(F32), MOE_TILE)
    pos2 = pos.reshape(m_tok, TOP_K)
    outs = []
    for x, row0 in zip(groups, row0s):
        n_rows = x.shape[0] * x.shape[1]
        outs.append(_combine(h1, top_w, pos2, ys, row0, n_rows).reshape(x.shape))
    return tuple(outs)
```
